```python
import math
import jax
import jax.numpy as jnp
from jax import lax
import numpy as np

D_MODEL = 1024
BATCH = 32
SEQ = 256
DEPTH = 4
DEC_BATCH = 8
DEC_SEQ = 2048
PAST_LEN = 256

GRID_W = 64
EPS = 1e-6
NEG_INF = -1e30

N_HEADS = 8
N_KV_HEADS = 2
GQA_GROUP = N_HEADS // N_KV_HEADS
HEAD_DIM = 64
ATTN_WIDTH = N_HEADS * HEAD_DIM
KV_WIDTH = N_KV_HEADS * HEAD_DIM
WINDOW = 128
BLOCK = 128
ROPE_THETA = 10000.0

SSD_HEADS = 8
SSD_HEADDIM = 64
SSD_WIDTH = SSD_HEADS * SSD_HEADDIM
SSD_GROUPS = 2
D_STATE = 64
BC_WIDTH = SSD_GROUPS * D_STATE
D_CONV = 5
CHUNK = 128
XBC_WIDTH = SSD_WIDTH + 2 * BC_WIDTH

MIX_WIDTH = ATTN_WIDTH + SSD_WIDTH
IN_WIDTH = ATTN_WIDTH + 2 * KV_WIDTH + XBC_WIDTH + SSD_WIDTH + 2 * SSD_HEADS

N_EXPERTS = 16
N_EXPERT_GROUPS = 4
EXPERTS_PER_GROUP = N_EXPERTS // N_EXPERT_GROUPS
TOP_K = 2
D_FF = 256

kernel_name = 'hybrid_diffusion_ssd_swa_moe_step'


def rmsnorm(x, w):
    xf = x.astype(jnp.float32)
    y = xf * lax.rsqrt(jnp.mean(xf * xf, axis=-1, keepdims=True) + EPS)
    return (y * w.astype(jnp.float32)).astype(x.dtype)


def adaln(cvec, w_ada_l, b_ada_l):
    m = jax.nn.silu(cvec) @ w_ada_l + b_ada_l
    return jnp.split(m[:, None, :], 6, axis=-1)


def modulate(h, shift, scale):
    return h * (1 + scale) + shift


def axial_rope_tables(n_tokens, dtype):
    rows = n_tokens // GRID_W
    row = jnp.broadcast_to(jnp.arange(rows, dtype=jnp.float32)[:, None], (rows, GRID_W)).reshape(-1)
    col = jnp.broadcast_to(jnp.arange(GRID_W, dtype=jnp.float32)[None, :], (rows, GRID_W)).reshape(-1)
    n_freq = HEAD_DIM // 4
    inv_freq = jnp.power(ROPE_THETA, -jnp.arange(n_freq, dtype=jnp.float32) / n_freq)
    ang = jnp.concatenate([row[:, None] * inv_freq, col[:, None] * inv_freq], axis=-1)
    return jnp.cos(ang).astype(dtype), jnp.sin(ang).astype(dtype)


def apply_rope(x, cos, sin):
    half = HEAD_DIM // 2
    x1, x2 = x[..., :half], x[..., half:]
    c = cos[None, :, None, :]
    s = sin[None, :, None, :]
    return jnp.concatenate([x1 * c - x2 * s, x1 * s + x2 * c], axis=-1)


def depthwise_conv_centred(u, w, b):
    pad = (D_CONV - 1) // 2
    y = lax.conv_general_dilated(u, w[:, None, :].astype(u.dtype), window_strides=(1,),
                                 padding=[(pad, pad)], dimension_numbers=('NWC', 'WIO', 'NWC'),
                                 feature_group_count=u.shape[-1])
    return y + b.astype(u.dtype)


def project(h, w_in_l, conv_w_l, conv_b_l):
    b, l, _ = h.shape
    p = h @ w_in_l
    o1 = ATTN_WIDTH
    o2 = o1 + KV_WIDTH
    o3 = o2 + KV_WIDTH
    o4 = o3 + XBC_WIDTH
    o5 = o4 + SSD_WIDTH
    q = p[..., :o1].reshape(b, l, N_HEADS, HEAD_DIM)
    k = p[..., o1:o2].reshape(b, l, N_KV_HEADS, HEAD_DIM)
    v = p[..., o2:o3].reshape(b, l, N_KV_HEADS, HEAD_DIM)
    xbc = jax.nn.silu(depthwise_conv_centred(p[..., o3:o4], conv_w_l, conv_b_l))
    z = p[..., o4:o5]
    dt_raw = p[..., o5:].reshape(b, l, 2, SSD_HEADS)
    xs = xbc[..., :SSD_WIDTH].reshape(b, l, SSD_HEADS, SSD_HEADDIM)
    rep = SSD_HEADS // SSD_GROUPS
    bm = jnp.repeat(xbc[..., SSD_WIDTH:SSD_WIDTH + BC_WIDTH].reshape(b, l, SSD_GROUPS, D_STATE), rep, axis=2)
    cm = jnp.repeat(xbc[..., SSD_WIDTH + BC_WIDTH:].reshape(b, l, SSD_GROUPS, D_STATE), rep, axis=2)
    return q, k, v, xs, bm, cm, z, dt_raw


def segsum(a):
    t = a.shape[-1]
    cs = jnp.cumsum(a, axis=-1)
    seg = cs[..., :, None] - cs[..., None, :]
    mask = jnp.tril(jnp.ones((t, t), dtype=bool))
    return jnp.where(mask, seg, -jnp.inf)


def ssd_scan(xs, dt, a, bm, cm, init_state):
    b, l, h, p = xs.shape
    nc = l // CHUNK
    f32 = jnp.float32
    x = (xs.astype(f32) * dt[..., None]).reshape(b, nc, CHUNK, h, p)
    bc = bm.astype(f32).reshape(b, nc, CHUNK, h, D_STATE)
    cc = cm.astype(f32).reshape(b, nc, CHUNK, h, D_STATE)
    da = (dt * a).reshape(b, nc, CHUNK, h).transpose(0, 3, 1, 2)
    da_cs = jnp.cumsum(da, axis=-1)
    decay_in = jnp.exp(segsum(da))
    scores = jnp.einsum('bclhn,bcshn->bhcls', cc, bc) * decay_in
    y_diag = jnp.einsum('bhcls,bcshp->bclhp', scores, x)
    decay_to_end = jnp.exp(da_cs[..., -1:] - da_cs)
    chunk_states = jnp.einsum('bclhn,bhcl,bclhp->bchpn', bc, decay_to_end, x)
    chunk_states = jnp.concatenate([init_state[:, None], chunk_states], axis=1)
    chunk_decay = jnp.exp(segsum(jnp.pad(da_cs[..., -1], ((0, 0), (0, 0), (1, 0)))))
    states = jnp.einsum('bhzc,bchpn->bzhpn', chunk_decay, chunk_states)
    y_off = jnp.einsum('bclhn,bchpn,bhcl->bclhp', cc, states[:, :-1], jnp.exp(da_cs))
    y = (y_diag + y_off).reshape(b, l, h, p)
    return y, states[:, -1]


def gated_rmsnorm(y, z, w):
    shape = y.shape
    g = y.astype(jnp.float32) * jax.nn.silu(z.astype(jnp.float32))
    g = g.reshape(shape[:-1] + (SSD_GROUPS, SSD_WIDTH // SSD_GROUPS))
    g = g * lax.rsqrt(jnp.mean(g * g, axis=-1, keepdims=True) + EPS)
    return g.reshape(shape) * w.astype(jnp.float32)


def ssd_bidirectional(xs, bm, cm, z, dt_raw, dt_bias_l, a_log_l, d_l, norm_w_l, init_f, init_b):
    b, l = xs.shape[:2]
    dt = jax.nn.softplus(dt_raw.astype(jnp.float32) + dt_bias_l.astype(jnp.float32))
    a = -jnp.exp(a_log_l.astype(jnp.float32))
    y_f, s_f = ssd_scan(xs, dt[:, :, 0], a[0], bm, cm, init_f)
    flip = lambda t: jnp.flip(t, axis=1)
    y_b, s_b = ssd_scan(flip(xs), flip(dt[:, :, 1]), a[1], flip(bm), flip(cm), init_b)
    y = y_f + flip(y_b) + d_l.astype(jnp.float32)[:, None] * xs.astype(jnp.float32)
    out = gated_rmsnorm(y.reshape(b, l, SSD_WIDTH), z, norm_w_l).astype(xs.dtype)
    return out, s_f, s_b


def sink_probs(scores, sink):
    s = sink.astype(jnp.float32)[:, :, None]
    m = s
    for sc in scores:
        m = jnp.maximum(m, sc.max(axis=-1))
    probs = [jnp.exp(sc - m[..., None]) for sc in scores]
    denom = jnp.exp(s - m)
    for pr in probs:
        denom = denom + pr.sum(axis=-1)
    inv = 1.0 / denom
    return [pr * inv[..., None] for pr in probs]


def attn_context(q, k, v, sink):
    b, s = q.shape[:2]
    nq = s // BLOCK
    scale = HEAD_DIM ** -0.5
    sink = sink.reshape(N_KV_HEADS, GQA_GROUP)
    qb = q.reshape(b, nq, BLOCK, N_KV_HEADS, GQA_GROUP, HEAD_DIM).transpose(1, 0, 2, 3, 4, 5)

    def one_block(qblk):
        sc = jnp.einsum('bqkgd,bskd->bkgqs', qblk, k).astype(jnp.float32) * scale
        (pr,) = sink_probs([sc], sink)
        return jnp.einsum('bkgqs,bskd->bqkgd', pr, v.astype(jnp.float32))

    out = lax.map(one_block, qb)
    return out.transpose(1, 0, 2, 3, 4, 5).reshape(b, s, ATTN_WIDTH).astype(q.dtype)


def attn_latent(q, k, v, kc, vc, sink):
    b, l = q.shape[:2]
    nb = l // BLOCK
    scale = HEAD_DIM ** -0.5
    sink = sink.reshape(N_KV_HEADS, GQA_GROUP)
    qb = q.reshape(b, nb, BLOCK, N_KV_HEADS, GQA_GROUP, HEAD_DIM)
    pad = ((0, 0), (BLOCK, BLOCK), (0, 0), (0, 0))
    kp = jnp.pad(k, pad).reshape(b, nb + 2, BLOCK, N_KV_HEADS, HEAD_DIM)
    vp = jnp.pad(v, pad).reshape(b, nb + 2, BLOCK, N_KV_HEADS, HEAD_DIM)
    kband = jnp.concatenate([kp[:, :-2], kp[:, 1:-1], kp[:, 2:]], axis=2)
    vband = jnp.concatenate([vp[:, :-2], vp[:, 1:-1], vp[:, 2:]], axis=2)
    s_band = jnp.einsum('bnqkgd,bnskd->bnkgqs', qb, kband).astype(jnp.float32) * scale
    s_ctx = jnp.einsum('bnqkgd,bpkd->bnkgqp', qb, kc.astype(q.dtype)).astype(jnp.float32) * scale
    qpos = jnp.arange(BLOCK)[:, None]
    kpos = jnp.arange(3 * BLOCK)[None, :] - BLOCK
    within = jnp.abs(qpos - kpos) <= WINDOW
    key_abs = jnp.arange(nb)[:, None] * BLOCK + kpos
    valid = (key_abs >= 0) & (key_abs < l)
    mask = within[None, :, :] & valid[:, None, :]
    s_band = jnp.where(mask[None, :, None, None, :, :], s_band, NEG_INF)
    p_ctx, p_band = sink_probs([s_ctx, s_band], sink)
    out = (jnp.einsum('bnkgqp,bpkd->bnqkgd', p_ctx, vc.astype(jnp.float32))
           + jnp.einsum('bnkgqs,bnskd->bnqkgd', p_band, vband.astype(jnp.float32)))
    return out.reshape(b, l, ATTN_WIDTH).astype(q.dtype)


def mixer_context(h, w_in_l, conv_w_l, conv_b_l, sink_l, dt_bias_l, a_log_l, d_l, ssd_norm_l, w_out_l):
    q, k, v, xs, bm, cm, z, dt_raw = project(h, w_in_l, conv_w_l, conv_b_l)
    b = h.shape[0]
    attn = attn_context(q, k, v, sink_l)
    zero = jnp.zeros((b, SSD_HEADS, SSD_HEADDIM, D_STATE), jnp.float32)
    ssd, s_f, s_b = ssd_bidirectional(xs, bm, cm, z, dt_raw, dt_bias_l, a_log_l, d_l, ssd_norm_l, zero, zero)
    out = jnp.concatenate([attn, ssd], axis=-1) @ w_out_l
    state = jnp.stack([s_f, s_b], axis=1).astype(h.dtype)
    return out, k, v, state


def mixer_latent(h, kc, vc, st, rope_cos, rope_sin, w_in_l, conv_w_l, conv_b_l, sink_l,
                 dt_bias_l, a_log_l, d_l, ssd_norm_l, w_out_l):
    q, k, v, xs, bm, cm, z, dt_raw = project(h, w_in_l, conv_w_l, conv_b_l)
    q = apply_rope(q, rope_cos, rope_sin)
    k = apply_rope(k, rope_cos, rope_sin)
    attn = attn_latent(q, k, v, kc, vc, sink_l)
    init_f = st[:, 0].astype(jnp.float32)
    init_b = st[:, 1].astype(jnp.float32)
    ssd, _, _ = ssd_bidirectional(xs, bm, cm, z, dt_raw, dt_bias_l, a_log_l, d_l, ssd_norm_l, init_f, init_b)
    return jnp.concatenate([attn, ssd], axis=-1) @ w_out_l


def moe_ffn(h, w_router, router_bias, w_gate_l, w_up_l, w_down_l):
    b, l, d = h.shape
    t = h.reshape(b * l, d)
    scores = jax.nn.sigmoid((t @ w_router).astype(jnp.float32))
    sel = scores + router_bias.astype(jnp.float32)
    grp = sel.reshape(-1, N_EXPERT_GROUPS, EXPERTS_PER_GROUP)
    grp_score = lax.top_k(grp, TOP_K)[0].sum(axis=-1)
    best = jnp.argmax(grp_score, axis=-1)
    in_group = (jnp.arange(N_EXPERTS) // EXPERTS_PER_GROUP)[None, :] == best[:, None]
    masked = jnp.where(in_group, sel, NEG_INF)
    _, idx = lax.top_k(masked, TOP_K)
    w = jnp.take_along_axis(scores, idx, axis=-1)
    w = w / w.sum(axis=-1, keepdims=True)
    gates = jnp.sum(jax.nn.one_hot(idx, N_EXPERTS, dtype=jnp.float32) * w[..., None], axis=1)
    hid = jax.nn.silu(jnp.einsum('nd,edf->enf', t, w_gate_l)) * jnp.einsum('nd,edf->enf', t, w_up_l)
    hid = hid * gates.T[:, :, None].astype(hid.dtype)
    out = jnp.einsum('enf,efd->nd', hid, w_down_l)
    return out.reshape(b, l, d)


def setup_inputs(seed: int = 0) -> dict:
    key = jax.random.key(seed)
    ks = jax.random.split(key, 26)
    f32 = jnp.float32

    def nrm(k, shape, scale):
        return jax.random.normal(k, shape, f32) * scale

    x_prompt = nrm(ks[0], (BATCH, SEQ, D_MODEL), 1.0)
    x_sample = nrm(ks[1], (DEC_BATCH, DEC_SEQ, D_MODEL), 1.0)
    cache_k = nrm(ks[2], (DEC_BATCH, DEPTH, PAST_LEN, N_KV_HEADS, HEAD_DIM), 1.0)
    cache_v = nrm(ks[3], (DEC_BATCH, DEPTH, PAST_LEN, N_KV_HEADS, HEAD_DIM), 1.0)
    state_ssm = nrm(ks[4], (DEC_BATCH, DEPTH, 2, SSD_HEADS, SSD_HEADDIM, D_STATE), 0.1)
    c = nrm(ks[5], (DEC_BATCH, D_MODEL), 1.0)
    c_ctx = nrm(ks[6], (D_MODEL,), 1.0)
    norm1_w = 1.0 + nrm(ks[7], (DEPTH, D_MODEL), 0.02)
    norm2_w = 1.0 + nrm(ks[8], (DEPTH, D_MODEL), 0.02)
    final_norm_w = 1.0 + nrm(ks[9], (D_MODEL,), 0.02)
    w_ada = nrm(ks[10], (DEPTH, D_MODEL, 6 * D_MODEL), 0.5 * D_MODEL ** -0.5)
    b_ada = nrm(ks[11], (DEPTH, 6 * D_MODEL), 0.02)
    w_in = nrm(ks[12], (DEPTH, D_MODEL, IN_WIDTH), D_MODEL ** -0.5)
    conv_w = nrm(ks[13], (DEPTH, D_CONV, XBC_WIDTH), D_CONV ** -0.5)
    conv_b = nrm(ks[14], (DEPTH, XBC_WIDTH), 0.01)
    attn_sink = nrm(ks[15], (DEPTH, N_HEADS), 0.5)
    dt0 = jnp.exp(jax.random.uniform(ks[16], (DEPTH, 2, SSD_HEADS), f32, math.log(1e-3), math.log(1e-1)))
    dt_bias = dt0 + jnp.log(-jnp.expm1(-dt0))
    a_log = jnp.log(jax.random.uniform(ks[17], (DEPTH, 2, SSD_HEADS), f32, 1.0, 16.0))
    d_skip = 1.0 + nrm(ks[18], (DEPTH, SSD_HEADS), 0.1)
    ssd_norm_w = 1.0 + nrm(ks[19], (DEPTH, SSD_WIDTH), 0.02)
    w_out = nrm(ks[20], (DEPTH, MIX_WIDTH, D_MODEL), MIX_WIDTH ** -0.5)
    w_router = nrm(ks[21], (D_MODEL, N_EXPERTS), D_MODEL ** -0.5)
    router_bias = nrm(ks[22], (N_EXPERTS,), 0.01)
    w_gate = nrm(ks[23], (DEPTH, N_EXPERTS, D_MODEL, D_FF), D_MODEL ** -0.5)
    w_up = nrm(ks[24], (DEPTH, N_EXPERTS, D_MODEL, D_FF), D_MODEL ** -0.5)
    w_down = nrm(ks[25], (DEPTH, N_EXPERTS, D_FF, D_MODEL), D_FF ** -0.5)
    return {'x_prompt': x_prompt, 'x_sample': x_sample, 'cache_k': cache_k, 'cache_v': cache_v,
            'state_ssm': state_ssm, 'c': c, 'c_ctx': c_ctx, 'norm1_w': norm1_w, 'norm2_w': norm2_w,
            'final_norm_w': final_norm_w, 'w_ada': w_ada, 'b_ada': b_ada, 'w_in': w_in,
            'conv_w': conv_w, 'conv_b': conv_b, 'attn_sink': attn_sink, 'dt_bias': dt_bias,
            'a_log': a_log, 'd_skip': d_skip, 'ssd_norm_w': ssd_norm_w, 'w_out': w_out,
            'w_router': w_router, 'router_bias': router_bias, 'w_gate': w_gate, 'w_up': w_up,
            'w_down': w_down}


def reference(x_prompt, x_sample, cache_k, cache_v, state_ssm, c, c_ctx,
              norm1_w, norm2_w, final_norm_w, w_ada, b_ada, w_in, conv_w, conv_b,
              attn_sink, dt_bias, a_log, d_skip, ssd_norm_w, w_out,
              w_router, router_bias, w_gate, w_up, w_down):
    xp = x_prompt
    xs = x_sample
    rope_cos, rope_sin = axial_rope_tables(xs.shape[1], xs.dtype)
    ctx_vec = c_ctx[None, :]
    new_k, new_v, new_s = [], [], []
    for l in range(DEPTH):
        sh1, sc1, g1, sh2, sc2, g2 = adaln(ctx_vec, w_ada[l], b_ada[l])
        h = modulate(rmsnorm(xp, norm1_w[l]), sh1, sc1)
        mix, k_ctx, v_ctx, s_ctx = mixer_context(h, w_in[l], conv_w[l], conv_b[l], attn_sink[l],
                                                 dt_bias[l], a_log[l], d_skip[l], ssd_norm_w[l], w_out[l])
        xp = xp + g1 * mix
        h = modulate(rmsnorm(xp, norm2_w[l]), sh2, sc2)
        xp = xp + g2 * moe_ffn(h, w_router, router_bias, w_gate[l], w_up[l], w_down[l])
        new_k.append(k_ctx)
        new_v.append(v_ctx)
        new_s.append(s_ctx)
        sh1, sc1, g1, sh2, sc2, g2 = adaln(c, w_ada[l], b_ada[l])
        h = modulate(rmsnorm(xs, norm1_w[l]), sh1, sc1)
        mix = mixer_latent(h, cache_k[:, l], cache_v[:, l], state_ssm[:, l], rope_cos, rope_sin,
                           w_in[l], conv_w[l], conv_b[l], attn_sink[l], dt_bias[l], a_log[l],
                           d_skip[l], ssd_norm_w[l], w_out[l])
        xs = xs + g1 * mix
        h = modulate(rmsnorm(xs, norm2_w[l]), sh2, sc2)
        xs = xs + g2 * moe_ffn(h, w_router, router_bias, w_gate[l], w_up[l], w_down[l])
    y_prompt = rmsnorm(xp, final_norm_w)
    y_sample = rmsnorm(xs, final_norm_w)
    new_cache_k = jnp.stack(new_k, axis=1)
    new_cache_v = jnp.stack(new_v, axis=1)
    new_state_ssm = jnp.stack(new_s, axis=1)
    return (y_prompt, y_sample, new_cache_k, new_cache_v, new_state_ssm)
```

```python
import functools

import numpy as np
import jax
import jax.numpy as jnp
from jax import lax
from jax.experimental import pallas as pl
from jax.experimental.pallas import tpu as pltpu

F32 = jnp.float32
BF16 = jnp.bfloat16

D_MODEL = 1024
DEPTH = 4
GRID_W = 64
EPS = 1e-6
NEG_INF = -1e30
N_HEADS = 8
N_KV_HEADS = 2
GQA_GROUP = N_HEADS // N_KV_HEADS
HEAD_DIM = 64
ATTN_WIDTH = N_HEADS * HEAD_DIM
KV_WIDTH = N_KV_HEADS * HEAD_DIM
BLOCK = 128
ROPE_THETA = 10000.0
SSD_HEADS = 8
SSD_HEADDIM = 64
SSD_WIDTH = SSD_HEADS * SSD_HEADDIM
SSD_GROUPS = 2
D_STATE = 64
BC_WIDTH = SSD_GROUPS * D_STATE
D_CONV = 5
CHUNK = 128
XBC_WIDTH = SSD_WIDTH + 2 * BC_WIDTH
N_EXPERTS = 16
N_EXPERT_GROUPS = 4
EXPERTS_PER_GROUP = N_EXPERTS // N_EXPERT_GROUPS
D_FF = 256
GROUP_SHIFT = EXPERTS_PER_GROUP.bit_length() - 1
assert 1 << GROUP_SHIFT == EXPERTS_PER_GROUP

LANES = 128
SUBLANES = 8
VMEM_LIMIT = 56 * 1024 * 1024

O_Q = 0
O_K = O_Q + ATTN_WIDTH
O_V = O_K + KV_WIDTH
O_XBC = O_V + KV_WIDTH
O_Z = O_XBC + XBC_WIDTH
O_DT = O_Z + SSD_WIDTH
IN_PAD = O_DT + LANES
MOD_ROWS = 16
ROW_TILE = 512
MOE_TILE = 1024


def _cparams(sem):
    return pltpu.CompilerParams(dimension_semantics=sem, vmem_limit_bytes=VMEM_LIMIT)


def _sigmoid(x):
    return 1.0 / (1.0 + jnp.exp(-x))


def _split2(x):
    hi = x.astype(BF16)
    lo = (x - hi.astype(F32)).astype(BF16)
    return hi, lo


def _split3(x):
    hi = x.astype(BF16)
    r = x - hi.astype(F32)
    mid = r.astype(BF16)
    lo = (r - mid.astype(F32)).astype(BF16)
    return hi, mid, lo


def _dot(a, b):
    return jnp.dot(a, b, preferred_element_type=F32)


def _dot_nt(a, b):
    return lax.dot_general(a, b, (((1,), (1,)), ((), ())), preferred_element_type=F32)


def _adaln_kernel(c_ref, w_ref, b_ref, o_ref):
    c = c_ref[...]
    s = c * _sigmoid(c)
    s_hi, s_lo = _split2(s)
    w = w_ref[...]
    w_hi, w_lo = _split2(w)
    acc = _dot(s_hi, w_hi) + _dot(s_lo, w_hi) + _dot(s_hi, w_lo)
    o_ref[...] = acc + b_ref[...]


def _adaln_all(cvec, w_ada, b_ada):
    depth, d, n = w_ada.shape
    tn = 1024
    return pl.pallas_call(
        _adaln_kernel,
        grid=(depth, n // tn),
        in_specs=[
            pl.BlockSpec((MOD_ROWS, d), lambda l, j: (0, 0)),
            pl.BlockSpec((None, d, tn), lambda l, j: (l, 0, j)),
            pl.BlockSpec((None, 1, tn), lambda l, j: (l, 0, j)),
        ],
        out_specs=pl.BlockSpec((None, MOD_ROWS, tn), lambda l, j: (l, 0, j)),
        out_shape=jax.ShapeDtypeStruct((depth, MOD_ROWS, n), F32),
        compiler_params=_cparams(("arbitrary", "arbitrary")),
        name="adaln",
    )(cvec, w_ada, b_ada.reshape(depth, 1, n))


def _mod_spec(layer, which, row_of_step):
    return pl.BlockSpec((None, None, None, 1, D_MODEL),
                        lambda i, *_: (layer, row_of_step(i), which, 0, 0))


def _rope(x, cos4, sin4):
    lane = lax.broadcasted_iota(jnp.int32, x.shape, 1)
    first_half = (lane & (HEAD_DIM - 1)) < (HEAD_DIM // 2)
    partner = jnp.where(first_half,
                        pltpu.roll(x, LANES - HEAD_DIM // 2, 1),
                        pltpu.roll(x, HEAD_DIM // 2, 1))
    return x * cos4 + partner * sin4


def _inproj_kernel(*refs, rope):
    if rope:
        (x_ref, nw_ref, sh_ref, sc_ref, w_ref, cos_ref, sin_ref,
         q_ref, k_ref, v_ref, xbc_ref, z_ref, dt_ref) = refs
    else:
        (x_ref, nw_ref, sh_ref, sc_ref, w_ref,
         q_ref, k_ref, v_ref, xbc_ref, z_ref, dt_ref) = refs
    x = x_ref[...]
    ms = jnp.mean(x * x, axis=-1, keepdims=True)
    y = x * lax.rsqrt(ms + EPS) * nw_ref[...]
    h = (y * (1.0 + sc_ref[...]) + sh_ref[...]).astype(BF16)

    def seg(a, b):
        return _dot(h, w_ref[:, a:b])

    if rope:
        cos4 = cos_ref[...]
        sin4 = sin_ref[...]
        for j in range(ATTN_WIDTH // LANES):
            q_ref[:, j * LANES:(j + 1) * LANES] = _rope(
                seg(O_Q + j * LANES, O_Q + (j + 1) * LANES), cos4, sin4).astype(q_ref.dtype)
        k_ref[...] = _rope(seg(O_K, O_V), cos4, sin4)
    else:
        q_ref[...] = seg(O_Q, O_K).astype(q_ref.dtype)
        k_ref[...] = seg(O_K, O_V)
    v_ref[...] = seg(O_V, O_XBC)
    xbc_ref[...] = seg(O_XBC, O_Z)
    z_ref[...] = seg(O_Z, O_DT)
    dt_ref[...] = seg(O_DT, IN_PAD)


def _in_projection(x, norm_w, mods, w_in_p, layer, row_of_step, rope_tabs, seq_len):
    t = x.shape[0]
    tm = ROW_TILE
    rope = rope_tabs is not None
    row = lambda i: (i, 0)
    in_specs = [
        pl.BlockSpec((tm, D_MODEL), row),
        pl.BlockSpec((None, 1, D_MODEL), lambda i: (layer, 0, 0)),
        _mod_spec(layer, 0, row_of_step),
        _mod_spec(layer, 1, row_of_step),
        pl.BlockSpec((None, D_MODEL, IN_PAD), lambda i: (layer, 0, 0)),
    ]
    args = [x, norm_w, mods, mods, w_in_p]
    if rope:
        steps_per_seq = seq_len // tm
        tab = pl.BlockSpec((tm, LANES), lambda i: (i % steps_per_seq, 0))
        in_specs += [tab, tab]
        args += list(rope_tabs)
    widths = (ATTN_WIDTH, KV_WIDTH, KV_WIDTH, XBC_WIDTH, SSD_WIDTH, LANES)
    dtypes = (BF16, F32, F32, F32, F32, F32)
    return pl.pallas_call(
        functools.partial(_inproj_kernel, rope=rope),
        grid=(t // tm,),
        in_specs=in_specs,
        out_specs=[pl.BlockSpec((tm, w), row) for w in widths],
        out_shape=[jax.ShapeDtypeStruct((t, w), dt) for w, dt in zip(widths, dtypes)],
        compiler_params=_cparams(("arbitrary",)),
        name="in_projection_rope" if rope else "in_projection",
    )(*args)


def _softmax_sink_pv(s, sink, v):
    m = jnp.maximum(jnp.max(s, axis=-1, keepdims=True), sink)
    p = jnp.exp(s - m)
    denom = jnp.sum(p, axis=-1, keepdims=True) + jnp.exp(sink - m)
    return _dot(p.astype(BF16), v) / denom


def _attn_ctx_kernel(sink_ref, q_ref, k_ref, v_ref, o_ref, *, layer):
    k = k_ref[...].astype(BF16)
    v = v_ref[...].astype(BF16)
    for pair in range(N_HEADS // 2):
        outs = []
        for h in (2 * pair, 2 * pair + 1):
            g = h // GQA_GROUP
            kv = slice(g * HEAD_DIM, (g + 1) * HEAD_DIM)
            s = _dot_nt(q_ref[:, h * HEAD_DIM:(h + 1) * HEAD_DIM], k[:, kv])
            outs.append(_softmax_sink_pv(s, sink_ref[layer, h], v[:, kv]))
        o_ref[:, pair * LANES:(pair + 1) * LANES] = jnp.concatenate(outs, axis=-1).astype(o_ref.dtype)


def _attention_context(q, k, v, attn_sink, layer, batch, seq):
    row = lambda b: (b, 0)
    return pl.pallas_call(
        functools.partial(_attn_ctx_kernel, layer=layer),
        grid=(batch,),
        in_specs=[
            pl.BlockSpec(memory_space=pltpu.SMEM),
            pl.BlockSpec((seq, ATTN_WIDTH), row),
            pl.BlockSpec((seq, KV_WIDTH), row),
            pl.BlockSpec((seq, KV_WIDTH), row),
        ],
        out_specs=pl.BlockSpec((seq, ATTN_WIDTH), row),
        out_shape=jax.ShapeDtypeStruct((batch * seq, ATTN_WIDTH), BF16),
        compiler_params=_cparams(("arbitrary",)),
        name="attention_context",
    )(attn_sink, q, k, v)


def _attn_lat_kernel(sink_ref, q_ref, k_ref, v_ref, kc_ref, vc_ref, o_ref, *, layer, n_blocks):
    n = pl.program_id(1)
    past = kc_ref.shape[0]
    prev = pl.multiple_of(jnp.maximum(n - 1, 0) * BLOCK, BLOCK)
    cur = pl.multiple_of(n * BLOCK, BLOCK)
    nxt = pl.multiple_of(jnp.minimum(n + 1, n_blocks - 1) * BLOCK, BLOCK)

    def gather(ctx_ref, seq_ref):
        return jnp.concatenate(
            [ctx_ref[...], seq_ref[pl.ds(prev, BLOCK), :], seq_ref[pl.ds(cur, BLOCK), :],
             seq_ref[pl.ds(nxt, BLOCK), :]], axis=0).astype(BF16)

    kcat = gather(kc_ref, k_ref)
    vcat = gather(vc_ref, v_ref)
    width = past + 3 * BLOCK
    qi = lax.broadcasted_iota(jnp.int32, (BLOCK, width), 0)
    col = lax.broadcasted_iota(jnp.int32, (BLOCK, width), 1)
    far = 4 * BLOCK
    in_prev = (col >= past) & (col < past + BLOCK)
    in_next = col >= past + 2 * BLOCK
    bad_prev = in_prev & (col - past < qi + jnp.where(n > 0, 0, far))
    bad_next = in_next & (col - (past + 2 * BLOCK) > qi - jnp.where(n < n_blocks - 1, 0, far))
    bias = jnp.where(bad_prev | bad_next, NEG_INF, 0.0)
    for pair in range(N_HEADS // 2):
        outs = []
        for h in (2 * pair, 2 * pair + 1):
            g = h // GQA_GROUP
            kv = slice(g * HEAD_DIM, (g + 1) * HEAD_DIM)
            s = _dot_nt(q_ref[:, h * HEAD_DIM:(h + 1) * HEAD_DIM], kcat[:, kv]) + bias
            outs.append(_softmax_sink_pv(s, sink_ref[layer, h], vcat[:, kv]))
        o_ref[:, pair * LANES:(pair + 1) * LANES] = jnp.concatenate(outs, axis=-1).astype(o_ref.dtype)


def _attention_latent(q, k, v, cache_k, cache_v, attn_sink, layer, batch, seq):
    n_blocks = seq // BLOCK
    past = cache_k.shape[2]
    ctx = pl.BlockSpec((None, None, past, KV_WIDTH), lambda b, n: (b, layer, 0, 0))
    whole = pl.BlockSpec((seq, KV_WIDTH), lambda b, n: (b, 0))
    qo = pl.BlockSpec((BLOCK, ATTN_WIDTH), lambda b, n: (b * n_blocks + n, 0))
    return pl.pallas_call(
        functools.partial(_attn_lat_kernel, layer=layer, n_blocks=n_blocks),
        grid=(batch, n_blocks),
        in_specs=[pl.BlockSpec(memory_space=pltpu.SMEM), qo, whole, whole, ctx, ctx],
        out_specs=qo,
        out_shape=jax.ShapeDtypeStruct((batch * seq, ATTN_WIDTH), BF16),
        compiler_params=_cparams(("arbitrary", "arbitrary")),
        name="attention_latent",
    )(attn_sink, q, k, v, cache_k, cache_v)


def _ssd_kernel(xbc_ref, z_ref, dt_ref, init_ref, cw_ref, cb_ref, dtb_ref, alog_ref, dsk_ref, nw_ref,
                tri_ref, exp_ref, y_ref, st_ref, pad_scr, act_scr, sb_scr, sf_scr, sr_scr, *, seq):
    n_chunks = seq // CHUNK
    half = SSD_WIDTH // SSD_GROUPS
    pad_lo = SUBLANES
    pad_scr[0:pad_lo, :] = jnp.zeros((pad_lo, XBC_WIDTH), F32)
    pad_scr[pad_lo:pad_lo + seq, :] = xbc_ref[...]
    pad_scr[pad_lo + seq:2 * pad_lo + seq, :] = jnp.zeros((pad_lo, XBC_WIDTH), F32)

    lane = lax.broadcasted_iota(jnp.int32, (CHUNK, LANES), 1)
    is_fwd = lane < SSD_HEADS
    is_head = lane < 2 * SSD_HEADS
    ti = lax.broadcasted_iota(jnp.int32, (CHUNK, CHUNK), 0)
    tj = lax.broadcasted_iota(jnp.int32, (CHUNK, CHUNK), 1)
    lower = tj <= ti
    upper = tj >= ti
    a_row = jnp.where(is_head[0:1, :], -jnp.exp(alog_ref[...]), 0.0)

    def conv_silu(r0):
        win = pad_scr[pl.ds(r0, CHUNK + 2 * pad_lo), :]
        first = pad_lo - (D_CONV - 1) // 2
        acc = cb_ref[...] + cw_ref[0:1, :] * win[first:first + CHUNK, :]
        for kk in range(1, D_CONV):
            acc = acc + cw_ref[kk:kk + 1, :] * win[first + kk:first + kk + CHUNK, :]
        return acc * _sigmoid(acc)

    def prep(r0):
        x = dt_ref[pl.ds(r0, CHUNK), :] + dtb_ref[...]
        dtv = jnp.maximum(x, 0.0) + jnp.log(1.0 + jnp.exp(-jnp.abs(x)))
        da = dtv * a_row
        hi, mid, lo = _split3(da)
        tri = tri_ref[...]
        cs = _dot(tri, hi) + _dot(tri, mid) + _dot(tri, lo)
        ecs = cs - da
        tot = cs[CHUNK - 1:CHUNK, :]
        ldt = jnp.log(dtv)
        col_q = jnp.where(is_fwd, cs, ecs)
        row_q = jnp.where(is_fwd, cs - ldt, ecs + ldt).T
        scale_y = jnp.where(is_fwd, jnp.exp(cs), jnp.exp(tot - ecs))
        scale_s = dtv * jnp.where(is_fwd, jnp.exp(tot - cs), jnp.exp(ecs))
        ex = exp_ref[...]
        scale_y = _dot(scale_y.astype(BF16), ex)
        scale_s = _dot(scale_s.astype(BF16), ex)
        d_hi, d_mid, d_lo = _split3(jnp.broadcast_to(jnp.exp(tot), (SUBLANES, LANES)))
        decay = (_dot(d_hi, ex) + _dot(d_mid, ex) + _dot(d_lo, ex))[0:1, :]
        return col_q, row_q, scale_y, scale_s, decay

    def chunk_state(act, weights, decay, state):
        xw = (act[:, 0:SSD_WIDTH] * weights).astype(BF16)
        bt = act[:, SSD_WIDTH:SSD_WIDTH + BC_WIDTH].T.astype(BF16)
        upd = jnp.concatenate(
            [_dot(bt[g * D_STATE:(g + 1) * D_STATE, :], xw[:, g * half:(g + 1) * half])
             for g in range(SSD_GROUPS)], axis=-1)
        return decay * state + upd

    sr_scr[...] = init_ref[1]

    def bwd_body(i, carry):
        c = n_chunks - 1 - i
        r0 = pl.multiple_of(c * CHUNK, CHUNK)
        act = conv_silu(r0)
        act_scr[pl.ds(r0, CHUNK), :] = act
        _, _, _, scale_s, decay = prep(r0)
        state = sr_scr[...]
        sb_scr[c] = state
        sr_scr[...] = chunk_state(act, scale_s[:, SSD_WIDTH:], decay[:, SSD_WIDTH:], state)
        return carry

    lax.fori_loop(0, n_chunks, bwd_body, 0)
    st_ref[1] = sr_scr[...]

    sf_scr[...] = init_ref[0]

    def fwd_body(c, carry):
        r0 = pl.multiple_of(c * CHUNK, CHUNK)
        act = act_scr[pl.ds(r0, CHUNK), :]
        col_q, row_q, scale_y, scale_s, decay = prep(r0)
        xs = act[:, 0:SSD_WIDTH]
        xs_b = xs.astype(BF16)
        bmat = act[:, SSD_WIDTH:SSD_WIDTH + BC_WIDTH].astype(BF16)
        cmat = act[:, SSD_WIDTH + BC_WIDTH:XBC_WIDTH].astype(BF16)
        s_f = sf_scr[...]
        s_b = sb_scr[c]
        y_parts = []
        for g in range(SSD_GROUPS):
            gs = slice(g * D_STATE, (g + 1) * D_STATE)
            cb = _dot_nt(cmat[:, gs], bmat[:, gs])
            for pair in range(SSD_HEADS // SSD_GROUPS // 2):
                ws = []
                for h in (g * 4 + 2 * pair, g * 4 + 2 * pair + 1):
                    hb = SSD_HEADS + h
                    e_f = jnp.exp(jnp.where(lower, col_q[:, h:h + 1] - row_q[h:h + 1, :], NEG_INF))
                    e_b = jnp.exp(jnp.where(upper, row_q[hb:hb + 1, :] - col_q[:, hb:hb + 1], NEG_INF))
                    ws.append((cb * (e_f + e_b)).astype(BF16))
                h0 = g * 4 + 2 * pair
                slab = xs_b[:, h0 * SSD_HEADDIM:(h0 + 2) * SSD_HEADDIM]
                first = lane < SSD_HEADDIM
                rhs = jnp.concatenate([jnp.where(first, slab, jnp.zeros_like(slab)),
                                       jnp.where(first, jnp.zeros_like(slab), slab)], axis=0)
                y_parts.append(_dot(jnp.concatenate(ws, axis=-1), rhs))
        y = jnp.concatenate(y_parts, axis=-1)
        off_f = jnp.concatenate(
            [_dot(cmat[:, g * D_STATE:(g + 1) * D_STATE], s_f[:, g * half:(g + 1) * half].astype(BF16))
             for g in range(SSD_GROUPS)], axis=-1)
        off_b = jnp.concatenate(
            [_dot(cmat[:, g * D_STATE:(g + 1) * D_STATE], s_b[:, g * half:(g + 1) * half].astype(BF16))
             for g in range(SSD_GROUPS)], axis=-1)
        y = y + off_f * scale_y[:, 0:SSD_WIDTH] + off_b * scale_y[:, SSD_WIDTH:] + dsk_ref[...] * xs
        zc = z_ref[pl.ds(r0, CHUNK), :]
        gated = y * (zc * _sigmoid(zc))
        outs = []
        for g in range(SSD_GROUPS):
            gg = gated[:, g * half:(g + 1) * half]
            outs.append(gg * lax.rsqrt(jnp.mean(gg * gg, axis=-1, keepdims=True) + EPS))
        y_ref[pl.ds(r0, CHUNK), :] = (jnp.concatenate(outs, axis=-1) * nw_ref[...]).astype(y_ref.dtype)
        sf_scr[...] = chunk_state(act, scale_s[:, 0:SSD_WIDTH], decay[:, 0:SSD_WIDTH], s_f)
        return carry

    lax.fori_loop(0, n_chunks, fwd_body, 0)
    st_ref[0] = sf_scr[...]


def _ssd_constants():
    r = np.arange(CHUNK)
    tri = (r[None, :] <= r[:, None]).astype(np.float32)
    cols = np.arange(2 * SSD_WIDTH)
    expand = (cols[None, :] // SSD_HEADDIM == np.arange(LANES)[:, None]).astype(np.float32)
    return jnp.asarray(tri, BF16), jnp.asarray(expand, BF16)


def _ssd(xbc, z, dt, init, conv_w, conv_b, dt_bias, a_log, d_skip, ssd_norm_w, layer, batch, seq):
    tri, expand = _ssd_constants()
    n_chunks = seq // CHUNK
    row = lambda b: (b, 0)
    lay = lambda b: (layer, 0, 0)
    const = lambda b: (0, 0)
    state_spec = pl.BlockSpec((None, 2, D_STATE, SSD_WIDTH), lambda b: (b, 0, 0, 0))
    return pl.pallas_call(
        functools.partial(_ssd_kernel, seq=seq),
        grid=(batch,),
        in_specs=[
            pl.BlockSpec((seq, XBC_WIDTH), row),
            pl.BlockSpec((seq, SSD_WIDTH), row),
            pl.BlockSpec((seq, LANES), row),
            state_spec,
            pl.BlockSpec((None, SUBLANES, XBC_WIDTH), lay),
            pl.BlockSpec((None, 1, XBC_WIDTH), lay),
            pl.BlockSpec((None, 1, LANES), lay),
            pl.BlockSpec((None, 1, LANES), lay),
            pl.BlockSpec((None, 1, SSD_WIDTH), lay),
            pl.BlockSpec((None, 1, SSD_WIDTH), lay),
            pl.BlockSpec((CHUNK, CHUNK), const),
            pl.BlockSpec((LANES, 2 * SSD_WIDTH), const),
        ],
        out_specs=[pl.BlockSpec((seq, SSD_WIDTH), row), state_spec],
        out_shape=[jax.ShapeDtypeStruct((batch * seq, SSD_WIDTH), BF16),
                   jax.ShapeDtypeStruct((batch, 2, D_STATE, SSD_WIDTH), F32)],
        scratch_shapes=[
            pltpu.VMEM((seq + 2 * SUBLANES, XBC_WIDTH), F32),
            pltpu.VMEM((seq, XBC_WIDTH), F32),
            pltpu.VMEM((n_chunks, D_STATE, SSD_WIDTH), F32),
            pltpu.VMEM((D_STATE, SSD_WIDTH), F32),
            pltpu.VMEM((D_STATE, SSD_WIDTH), F32),
        ],
        compiler_params=_cparams(("arbitrary",)),
        name="ssd",
    )(xbc, z, dt, init, conv_w, conv_b, dt_bias, a_log, d_skip, ssd_norm_w, tri, expand)


def _first_argmax(rows):
    best_v = rows[0]
    best_i = jnp.zeros(rows[0].shape, jnp.int32)
    for i in range(1, len(rows)):
        better = rows[i] > best_v
        best_v = jnp.where(better, rows[i], best_v)
        best_i = jnp.where(better, i, best_i)
    return best_i, best_v


def _outproj_kernel(attn_ref, ssd_ref, x_ref, w_ref, g1_ref, nw_ref, sh_ref, sc_ref, wr_ref, rb_ref,
                    x1_ref, h_ref, gates_ref):
    half = ATTN_WIDTH
    mix = _dot(attn_ref[...], w_ref[0:half, :]) + _dot(ssd_ref[...], w_ref[half:, :])
    x1 = x_ref[...] + g1_ref[...] * mix
    x1_ref[...] = x1
    ms = jnp.mean(x1 * x1, axis=-1, keepdims=True)
    y = x1 * lax.rsqrt(ms + EPS) * nw_ref[...]
    h = y * (1.0 + sc_ref[...]) + sh_ref[...]
    h_hi, h_lo = _split2(h)
    h_ref[...] = h_hi
    wr = wr_ref[...]
    l1 = _dot(h_hi, wr).T
    l2 = _dot(h_lo, wr).T
    ne = N_EXPERTS
    logits = l1[0:ne, :] + l1[ne:2 * ne, :] + l2[0:ne, :]
    scores = _sigmoid(logits)
    sel = scores + rb_ref[...]
    srow = [sel[e:e + 1, :] for e in range(ne)]
    group_score = []
    for g in range(N_EXPERT_GROUPS):
        r = srow[g * EXPERTS_PER_GROUP:(g + 1) * EXPERTS_PER_GROUP]
        pairs = [r[i] + r[j] for i in range(len(r)) for j in range(i + 1, len(r))]
        top2 = pairs[0]
        for p in pairs[1:]:
            top2 = jnp.maximum(top2, p)
        group_score.append(top2)
    best_group, _ = _first_argmax(group_score)
    eid = lax.broadcasted_iota(jnp.int32, sel.shape, 0)
    masked = jnp.where(lax.shift_right_logical(eid, GROUP_SHIFT) == best_group, sel, NEG_INF)
    i1, _ = _first_argmax([masked[e:e + 1, :] for e in range(ne)])
    masked2 = jnp.where(eid == i1, -jnp.inf, masked)
    i2, _ = _first_argmax([masked2[e:e + 1, :] for e in range(ne)])
    pick1 = eid == i1
    pick2 = eid == i2
    w1 = jnp.sum(jnp.where(pick1, scores, 0.0), axis=0, keepdims=True)
    w2 = jnp.sum(jnp.where(pick2, scores, 0.0), axis=0, keepdims=True)
    wsum = w1 + w2
    gates = jnp.where(pick1, w1 / wsum, 0.0) + jnp.where(pick2, w2 / wsum, 0.0)
    pad = jnp.zeros((LANES - ne, gates.shape[1]), F32)
    gates_ref[...] = jnp.concatenate([gates, pad], axis=0).T


def _out_projection(attn, ssd, x, w_out_b, norm_w, mods, wr_p, rb_col, layer, row_of_step):
    t = x.shape[0]
    tm = ROW_TILE
    row = lambda i: (i, 0)
    return pl.pallas_call(
        _outproj_kernel,
        grid=(t // tm,),
        in_specs=[
            pl.BlockSpec((tm, ATTN_WIDTH), row),
            pl.BlockSpec((tm, SSD_WIDTH), row),
            pl.BlockSpec((tm, D_MODEL), row),
            pl.BlockSpec((None, D_MODEL, D_MODEL), lambda i: (layer, 0, 0)),
            _mod_spec(layer, 2, row_of_step),
            pl.BlockSpec((None, 1, D_MODEL), lambda i: (layer, 0, 0)),
            _mod_spec(layer, 3, row_of_step),
            _mod_spec(layer, 4, row_of_step),
            pl.BlockSpec((D_MODEL, LANES), lambda i: (0, 0)),
            pl.BlockSpec((N_EXPERTS, 1), lambda i: (0, 0)),
        ],
        out_specs=[pl.BlockSpec((tm, D_MODEL), row), pl.BlockSpec((tm, D_MODEL), row),
                   pl.BlockSpec((tm, LANES), row)],
        out_shape=[jax.ShapeDtypeStruct((t, D_MODEL), F32), jax.ShapeDtypeStruct((t, D_MODEL), BF16),
                   jax.ShapeDtypeStruct((t, LANES), F32)],
        compiler_params=_cparams(("arbitrary",)),
        name="out_projection",
    )(attn, ssd, x, w_out_b, mods, norm_w, mods, mods, wr_p, rb_col)


def _moe_kernel(h_ref, gates_ref, x1_ref, g2_ref, wg_ref, wu_ref, wd_ref, fw_ref, o_ref, *, final):
    e = pl.program_id(1)

    @pl.when(e == 0)
    def _():
        o_ref[...] = jnp.zeros(o_ref.shape, o_ref.dtype)

    h = h_ref[...]
    gate_act = _dot(h, wg_ref[...].astype(BF16))
    up = _dot(h, wu_ref[...].astype(BF16))
    lane = lax.broadcasted_iota(jnp.int32, gates_ref.shape, 1)
    ge = jnp.sum(jnp.where(lane == e, gates_ref[...], 0.0), axis=-1, keepdims=True)
    hid = gate_act * _sigmoid(gate_act) * up * ge
    o_ref[...] += _dot(hid.astype(BF16), wd_ref[...].astype(BF16))

    @pl.when(e == N_EXPERTS - 1)
    def _():
        x2 = x1_ref[...] + g2_ref[...] * o_ref[...]
        if final:
            ms = jnp.mean(x2 * x2, axis=-1, keepdims=True)
            x2 = x2 * lax.rsqrt(ms + EPS) * fw_ref[...]
        o_ref[...] = x2


def _moe(h, gates, x1, mods, w_gate, w_up, w_down, final_w, layer, row_of_step, final):
    t = h.shape[0]
    tm = MOE_TILE
    row = lambda i, e: (i, 0)
    return pl.pallas_call(
        functools.partial(_moe_kernel, final=final),
        grid=(t // tm, N_EXPERTS),
        in_specs=[
            pl.BlockSpec((tm, D_MODEL), row),
            pl.BlockSpec((tm, LANES), row),
            pl.BlockSpec((tm, D_MODEL), row),
            _mod_spec(layer, 5, lambda i: row_of_step(i)),
            pl.BlockSpec((None, None, D_MODEL, D_FF), lambda i, e: (layer, e, 0, 0)),
            pl.BlockSpec((None, None, D_MODEL, D_FF), lambda i, e: (layer, e, 0, 0)),
            pl.BlockSpec((None, None, D_FF, D_MODEL), lambda i, e: (layer, e, 0, 0)),
            pl.BlockSpec((1, D_MODEL), lambda i, e: (0, 0)),
        ],
        out_specs=pl.BlockSpec((tm, D_MODEL), row),
        out_shape=jax.ShapeDtypeStruct((t, D_MODEL), F32),
        compiler_params=_cparams(("arbitrary", "arbitrary")),
        name="moe_final" if final else "moe",
    )(h, gates, x1, mods, w_gate, w_up, w_down, final_w)


def _rope_tables(n_tokens):
    rows = n_tokens // GRID_W
    row = jnp.broadcast_to(jnp.arange(rows, dtype=F32)[:, None], (rows, GRID_W)).reshape(-1)
    col = jnp.broadcast_to(jnp.arange(GRID_W, dtype=F32)[None, :], (rows, GRID_W)).reshape(-1)
    n_freq = HEAD_DIM // 4
    inv_freq = jnp.power(ROPE_THETA, -jnp.arange(n_freq, dtype=F32) / n_freq)
    ang = jnp.concatenate([row[:, None] * inv_freq, col[:, None] * inv_freq], axis=-1)
    cos, sin = jnp.cos(ang), jnp.sin(ang)
    return jnp.tile(cos, (1, 4)), jnp.concatenate([-sin, sin, -sin, sin], axis=-1)


def kernel(x_prompt, x_sample, cache_k, cache_v, state_ssm, c, c_ctx, norm1_w, norm2_w, final_norm_w,
           w_ada, b_ada, w_in, conv_w, conv_b, attn_sink, dt_bias, a_log, d_skip, ssd_norm_w, w_out,
           w_router, router_bias, w_gate, w_up, w_down):
    bc, sc, d = x_prompt.shape
    bl, sl, _ = x_sample.shape
    depth = w_in.shape[0]
    assert d == D_MODEL and depth == DEPTH and bl + 1 <= MOD_ROWS
    assert sc % ROW_TILE == 0 or ROW_TILE % sc == 0
    assert sl % MOE_TILE == 0 and (bc * sc) % MOE_TILE == 0 and MOE_TILE % sc == 0

    cvec = jnp.zeros((MOD_ROWS, d), F32).at[0].set(c_ctx).at[1:1 + bl].set(c)
    mods = _adaln_all(cvec, w_ada, b_ada).reshape(depth, MOD_ROWS, 6, 1, d)

    scale = HEAD_DIM ** -0.5
    w_in_p = jnp.concatenate(
        [w_in[:, :, :ATTN_WIDTH] * scale, w_in[:, :, ATTN_WIDTH:],
         jnp.zeros((depth, d, IN_PAD - w_in.shape[2]), F32)], axis=-1).astype(BF16)
    w_out_b = w_out.astype(BF16)
    wr_hi = w_router.astype(BF16)
    wr_lo = (w_router - wr_hi.astype(F32)).astype(BF16)
    wr_p = jnp.concatenate([wr_hi, wr_lo, jnp.zeros((d, LANES - 2 * N_EXPERTS), BF16)], axis=-1)
    rb_col = router_bias.reshape(N_EXPERTS, 1)
    n1 = norm1_w.reshape(depth, 1, d)
    n2 = norm2_w.reshape(depth, 1, d)
    fw = final_norm_w.reshape(1, d)
    conv_w_p = jnp.concatenate([conv_w, jnp.zeros((depth, SUBLANES - D_CONV, XBC_WIDTH), F32)], axis=1)
    conv_b_p = conv_b.reshape(depth, 1, XBC_WIDTH)
    pad16 = lambda a: jnp.concatenate(
        [a.reshape(depth, 1, 2 * SSD_HEADS), jnp.zeros((depth, 1, LANES - 2 * SSD_HEADS), F32)], axis=-1)
    dtb_p = pad16(dt_bias)
    alog_p = pad16(a_log)
    dsk_p = jnp.repeat(d_skip, SSD_HEADDIM, axis=-1).reshape(depth, 1, SSD_WIDTH)
    snw_p = ssd_norm_w.reshape(depth, 1, SSD_WIDTH)
    rope_tabs = _rope_tables(sl)
    ck = cache_k.reshape(bl, depth, cache_k.shape[2], KV_WIDTH)
    cv = cache_v.reshape(bl, depth, cache_v.shape[2], KV_WIDTH)
    st_in = jnp.transpose(state_ssm, (0, 1, 2, 5, 3, 4)).reshape(bl, depth, 2, D_STATE, SSD_WIDTH)
    zero_state = jnp.zeros((bc, 2, D_STATE, SSD_WIDTH), F32)

    ctx_row = lambda i: 0
    lat_row_proj = lambda i: 1 + i // (sl // ROW_TILE)
    lat_row_moe = lambda i: 1 + i // (sl // MOE_TILE)

    xp = x_prompt.reshape(bc * sc, d)
    xs = x_sample.reshape(bl * sl, d)
    new_k, new_v, new_s = [], [], []
    for l in range(depth):
        last = l == depth - 1
        q, k, v, xbc, z, dt = _in_projection(xp, n1, mods, w_in_p, l, ctx_row, None, sc)
        attn = _attention_context(q, k, v, attn_sink, l, bc, sc)
        ssd, s_ctx = _ssd(xbc, z, dt, zero_state, conv_w_p, conv_b_p, dtb_p, alog_p, dsk_p, snw_p, l, bc, sc)
        x1, h2, gates = _out_projection(attn, ssd, xp, w_out_b, n2, mods, wr_p, rb_col, l, ctx_row)
        xp = _moe(h2, gates, x1, mods, w_gate, w_up, w_down, fw, l, ctx_row, last)
        new_k.append(k.reshape(bc, sc, N_KV_HEADS, HEAD_DIM))
        new_v.append(v.reshape(bc, sc, N_KV_HEADS, HEAD_DIM))
        new_s.append(jnp.transpose(s_ctx.reshape(bc, 2, D_STATE, SSD_HEADS, SSD_HEADDIM), (0, 1, 3, 4, 2)))
        q, k, v, xbc, z, dt = _in_projection(xs, n1, mods, w_in_p, l, lat_row_proj, rope_tabs, sl)
        attn = _attention_latent(q, k, v, ck, cv, attn_sink, l, bl, sl)
        ssd, _ = _ssd(xbc, z, dt, st_in[:, l], conv_w_p, conv_b_p, dtb_p, alog_p, dsk_p, snw_p, l, bl, sl)
        x1, h2, gates = _out_projection(attn, ssd, xs, w_out_b, n2, mods, wr_p, rb_col, l, lat_row_proj)
        xs = _moe(h2, gates, x1, mods, w_gate, w_up, w_down, fw, l, lat_row_moe, last)
    return (xp.reshape(bc, sc, d), xs.reshape(bl, sl, d),
            jnp.stack(new_k, axis=1), jnp.stack(new_v, axis=1), jnp.stack(new_s, axis=1))
```

```python
import functools

import numpy as np
import jax
import jax.numpy as jnp
from jax import lax
from jax.experimental import pallas as pl
from jax.experimental.pallas import tpu as pltpu

F32 = jnp.float32
BF16 = jnp.bfloat16

D_MODEL = 1024
DEPTH = 4
GRID_W = 64
EPS = 1e-6
NEG_INF = -1e30
N_HEADS = 8
N_KV_HEADS = 2
GQA_GROUP = N_HEADS // N_KV_HEADS
HEAD_DIM = 64
ATTN_WIDTH = N_HEADS * HEAD_DIM
KV_WIDTH = N_KV_HEADS * HEAD_DIM
BLOCK = 128
ROPE_THETA = 10000.0
SSD_HEADS = 8
SSD_HEADDIM = 64
SSD_WIDTH = SSD_HEADS * SSD_HEADDIM
SSD_GROUPS = 2
D_STATE = 64
BC_WIDTH = SSD_GROUPS * D_STATE
D_CONV = 5
CHUNK = 128
XBC_WIDTH = SSD_WIDTH + 2 * BC_WIDTH
N_EXPERTS = 16
N_EXPERT_GROUPS = 4
EXPERTS_PER_GROUP = N_EXPERTS // N_EXPERT_GROUPS
D_FF = 256
GROUP_SHIFT = EXPERTS_PER_GROUP.bit_length() - 1
assert 1 << GROUP_SHIFT == EXPERTS_PER_GROUP

LANES = 128
SUBLANES = 8
VMEM_LIMIT = 56 * 1024 * 1024

O_Q = 0
O_K = O_Q + ATTN_WIDTH
O_V = O_K + KV_WIDTH
O_XBC = O_V + KV_WIDTH
O_Z = O_XBC + XBC_WIDTH
O_DT = O_Z + SSD_WIDTH
IN_PAD = O_DT + LANES
MOD_ROWS = 16
ROW_TILE = 512
MOE_TILE = 1024
LAT_QBLOCKS = 4
LOG2E = 1.4426950408889634


def _cparams(sem):
    return pltpu.CompilerParams(dimension_semantics=sem, vmem_limit_bytes=VMEM_LIMIT)


def _sigmoid(x):
    return 1.0 / (1.0 + jnp.exp(-x))


def _split2(x):
    hi = x.astype(BF16)
    lo = (x - hi.astype(F32)).astype(BF16)
    return hi, lo


def _split3(x):
    hi = x.astype(BF16)
    r = x - hi.astype(F32)
    mid = r.astype(BF16)
    lo = (r - mid.astype(F32)).astype(BF16)
    return hi, mid, lo


def _dot(a, b):
    return jnp.dot(a, b, preferred_element_type=F32)


def _dot_nt(a, b):
    return lax.dot_general(a, b, (((1,), (1,)), ((), ())), preferred_element_type=F32)


def _adaln_kernel(c_ref, w_ref, b_ref, o_ref):
    c = c_ref[...]
    s = c * _sigmoid(c)
    s_hi, s_lo = _split2(s)
    w = w_ref[...]
    w_hi, w_lo = _split2(w)
    acc = _dot(s_hi, w_hi) + _dot(s_lo, w_hi) + _dot(s_hi, w_lo)
    o_ref[...] = acc + b_ref[...]


def _adaln_all(cvec, w_ada, b_ada):
    depth, d, n = w_ada.shape
    tn = 1024
    return pl.pallas_call(
        _adaln_kernel,
        grid=(depth, n // tn),
        in_specs=[
            pl.BlockSpec((MOD_ROWS, d), lambda l, j: (0, 0)),
            pl.BlockSpec((None, d, tn), lambda l, j: (l, 0, j)),
            pl.BlockSpec((None, 1, tn), lambda l, j: (l, 0, j)),
        ],
        out_specs=pl.BlockSpec((None, MOD_ROWS, tn), lambda l, j: (l, 0, j)),
        out_shape=jax.ShapeDtypeStruct((depth, MOD_ROWS, n), F32),
        compiler_params=_cparams(("arbitrary", "arbitrary")),
        name="adaln",
    )(cvec, w_ada, b_ada.reshape(depth, 1, n))


def _mod_spec(layer, which, row_of_step):
    return pl.BlockSpec((None, None, None, 1, D_MODEL),
                        lambda i, *_: (layer, row_of_step(i), which, 0, 0))


def _rope(x, cos4, sin4):
    lane = lax.broadcasted_iota(jnp.int32, x.shape, 1)
    first_half = (lane & (HEAD_DIM - 1)) < (HEAD_DIM // 2)
    partner = jnp.where(first_half,
                        pltpu.roll(x, LANES - HEAD_DIM // 2, 1),
                        pltpu.roll(x, HEAD_DIM // 2, 1))
    return x * cos4 + partner * sin4


def _inproj_kernel(*refs, rope):
    if rope:
        (x_ref, nw_ref, sh_ref, sc_ref, w_ref, cos_ref, sin_ref,
         q_ref, k_ref, v_ref, xbc_ref, z_ref, dt_ref) = refs
    else:
        (x_ref, nw_ref, sh_ref, sc_ref, w_ref,
         q_ref, k_ref, v_ref, xbc_ref, z_ref, dt_ref) = refs
    x = x_ref[...]
    ms = jnp.mean(x * x, axis=-1, keepdims=True)
    y = x * lax.rsqrt(ms + EPS) * nw_ref[...]
    h = (y * (1.0 + sc_ref[...]) + sh_ref[...]).astype(BF16)

    def seg(a, b):
        return _dot(h, w_ref[:, a:b])

    if rope:
        cos4 = cos_ref[...]
        sin4 = sin_ref[...]
        for j in range(ATTN_WIDTH // LANES):
            q_ref[:, j * LANES:(j + 1) * LANES] = _rope(
                seg(O_Q + j * LANES, O_Q + (j + 1) * LANES), cos4, sin4).astype(q_ref.dtype)
        k_ref[...] = _rope(seg(O_K, O_V), cos4, sin4)
    else:
        q_ref[...] = seg(O_Q, O_K).astype(q_ref.dtype)
        k_ref[...] = seg(O_K, O_V)
    v_ref[...] = seg(O_V, O_XBC)
    xbc_ref[...] = seg(O_XBC, O_Z)
    z_ref[...] = seg(O_Z, O_DT)
    dt_ref[...] = seg(O_DT, IN_PAD)


def _in_projection(x, norm_w, mods, w_in_p, layer, row_of_step, rope_tabs, seq_len):
    t = x.shape[0]
    tm = ROW_TILE
    rope = rope_tabs is not None
    row = lambda i: (i, 0)
    in_specs = [
        pl.BlockSpec((tm, D_MODEL), row),
        pl.BlockSpec((None, 1, D_MODEL), lambda i: (layer, 0, 0)),
        _mod_spec(layer, 0, row_of_step),
        _mod_spec(layer, 1, row_of_step),
        pl.BlockSpec((None, D_MODEL, IN_PAD), lambda i: (layer, 0, 0)),
    ]
    args = [x, norm_w, mods, mods, w_in_p]
    if rope:
        steps_per_seq = seq_len // tm
        tab = pl.BlockSpec((tm, LANES), lambda i: (i % steps_per_seq, 0))
        in_specs += [tab, tab]
        args += list(rope_tabs)
    widths = (ATTN_WIDTH, KV_WIDTH, KV_WIDTH, XBC_WIDTH, SSD_WIDTH, LANES)
    dtypes = (BF16, F32, F32, F32, F32, F32)
    return pl.pallas_call(
        functools.partial(_inproj_kernel, rope=rope),
        grid=(t // tm,),
        in_specs=in_specs,
        out_specs=[pl.BlockSpec((tm, w), row) for w in widths],
        out_shape=[jax.ShapeDtypeStruct((t, w), dt) for w, dt in zip(widths, dtypes)],
        compiler_params=_cparams(("arbitrary",)),
        name="in_projection_rope" if rope else "in_projection",
    )(*args)


def _kv_operands(k, v):
    low = lax.broadcasted_iota(jnp.int32, k.shape, 1) < HEAD_DIM
    k_sw = pltpu.roll(k, HEAD_DIM, 1)
    v_sw = pltpu.roll(v, HEAD_DIM, 1)
    kdup = (jnp.where(low, k, k_sw).astype(BF16), jnp.where(low, k_sw, k).astype(BF16))
    vdup = (jnp.where(low, v, v_sw).astype(BF16), jnp.where(low, v_sw, v).astype(BF16))
    return kdup, vdup


def _attend_group(sink_ref, q_ref, o_ref, layer, g, r0, tq, segments):
    low = lax.broadcasted_iota(jnp.int32, (tq, LANES), 1) < HEAD_DIM
    base = g * GQA_GROUP * HEAD_DIM
    rows = []
    for j in range(GQA_GROUP // 2):
        slab = q_ref[r0:r0 + tq, base + j * LANES:base + (j + 1) * LANES]
        zero = jnp.zeros_like(slab)
        rows += [jnp.where(low, slab, zero), jnp.where(low, zero, slab)]
    lhs = jnp.concatenate(rows, axis=0)
    sink = jnp.concatenate(
        [jnp.full((tq, LANES), sink_ref[layer, g * GQA_GROUP + i] * LOG2E, F32) for i in range(GQA_GROUP)],
        axis=0)
    tiles = []
    for kdup, _, patches in segments:
        s = _dot_nt(lhs, kdup)
        ts = [s[:, c * LANES:(c + 1) * LANES] for c in range(s.shape[1] // LANES)]
        for c, mask in patches or ():
            ts[c] = (ts[c].reshape(GQA_GROUP, tq, LANES) + mask[None]).reshape(ts[c].shape)
        tiles.append(ts)
    flat = [t for ts in tiles for t in ts]
    m = jnp.max(functools.reduce(jnp.maximum, flat + [sink]), axis=-1, keepdims=True)
    probs = [[jnp.exp2(t - m) for t in ts] for ts in tiles]
    total = functools.reduce(jnp.add, [p for ps in probs for p in ps])
    lane0 = lax.broadcasted_iota(jnp.int32, sink.shape, 1) == 0
    total = total + jnp.where(lane0, jnp.exp2(sink - m), 0.0)
    denom = jnp.sum(total, axis=-1, keepdims=True)
    o = None
    for ps, (_, vdup, _) in zip(probs, segments):
        part = _dot(jnp.concatenate(ps, axis=-1).astype(BF16), vdup)
        o = part if o is None else o + part
    o = o * (1.0 / denom)
    for j in range(GQA_GROUP // 2):
        pair = jnp.where(low, o[2 * j * tq:(2 * j + 1) * tq], o[(2 * j + 1) * tq:(2 * j + 2) * tq])
        o_ref[r0:r0 + tq, base + j * LANES:base + (j + 1) * LANES] = pair.astype(o_ref.dtype)


def _attn_ctx_kernel(sink_ref, q_ref, k_ref, v_ref, o_ref, *, layer):
    kdup, vdup = _kv_operands(k_ref[...], v_ref[...])
    for x in range(q_ref.shape[0] // BLOCK):
        for g in range(N_KV_HEADS):
            _attend_group(sink_ref, q_ref, o_ref, layer, g, x * BLOCK, BLOCK, [(kdup[g], vdup[g], None)])


def _attention_context(q, k, v, attn_sink, layer, batch, seq):
    row = lambda b: (b, 0)
    return pl.pallas_call(
        functools.partial(_attn_ctx_kernel, layer=layer),
        grid=(batch,),
        in_specs=[
            pl.BlockSpec(memory_space=pltpu.SMEM),
            pl.BlockSpec((seq, ATTN_WIDTH), row),
            pl.BlockSpec((seq, KV_WIDTH), row),
            pl.BlockSpec((seq, KV_WIDTH), row),
        ],
        out_specs=pl.BlockSpec((seq, ATTN_WIDTH), row),
        out_shape=jax.ShapeDtypeStruct((batch * seq, ATTN_WIDTH), BF16),
        compiler_params=_cparams(("arbitrary",)),
        name="attention_context",
    )(attn_sink, q, k, v)


def _attn_lat_kernel(sink_ref, q_ref, k_ref, v_ref, kc_ref, vc_ref, o_ref, *, layer, n_blocks):
    first_block = pl.program_id(1) * LAT_QBLOCKS
    band_blocks = LAT_QBLOCKS + 2

    def band(seq_ref):
        parts = []
        for i in range(band_blocks):
            blk = jnp.clip(first_block - 1 + i, 0, n_blocks - 1)
            parts.append(seq_ref[pl.ds(pl.multiple_of(blk * BLOCK, BLOCK), BLOCK), :])
        return jnp.concatenate(parts, axis=0)

    past_tiles = kc_ref.shape[0] // BLOCK
    kdup, vdup = _kv_operands(jnp.concatenate([kc_ref[...], band(k_ref)], axis=0),
                              jnp.concatenate([vc_ref[...], band(v_ref)], axis=0))
    qi = lax.broadcasted_iota(jnp.int32, (BLOCK, BLOCK), 0)
    kj = lax.broadcasted_iota(jnp.int32, (BLOCK, BLOCK), 1)
    for x in range(LAT_QBLOCKS):
        blk = first_block + x
        mask_prev = jnp.where(kj >= qi + jnp.where(blk > 0, 0, BLOCK), 0.0, NEG_INF)
        mask_next = jnp.where(kj <= qi - jnp.where(blk < n_blocks - 1, 0, BLOCK), 0.0, NEG_INF)
        patches = [(past_tiles, mask_prev), (past_tiles + 2, mask_next)]
        for g in range(N_KV_HEADS):
            if LAT_QBLOCKS == 1:
                keys, vals = kdup[g], vdup[g]
            else:
                pick = lambda a: jnp.concatenate(
                    [a[:past_tiles * BLOCK], a[(past_tiles + x) * BLOCK:(past_tiles + x + 3) * BLOCK]], axis=0)
                keys, vals = pick(kdup[g]), pick(vdup[g])
            _attend_group(sink_ref, q_ref, o_ref, layer, g, x * BLOCK, BLOCK, [(keys, vals, patches)])


def _attention_latent(q, k, v, cache_k, cache_v, attn_sink, layer, batch, seq):
    n_blocks = seq // BLOCK
    steps = n_blocks // LAT_QBLOCKS
    tq = LAT_QBLOCKS * BLOCK
    past = cache_k.shape[2]
    ctx = pl.BlockSpec((None, None, past, KV_WIDTH), lambda b, n: (b, layer, 0, 0))
    whole = pl.BlockSpec((seq, KV_WIDTH), lambda b, n: (b, 0))
    qo = pl.BlockSpec((tq, ATTN_WIDTH), lambda b, n: (b * steps + n, 0))
    return pl.pallas_call(
        functools.partial(_attn_lat_kernel, layer=layer, n_blocks=n_blocks),
        grid=(batch, steps),
        in_specs=[pl.BlockSpec(memory_space=pltpu.SMEM), qo, whole, whole, ctx, ctx],
        out_specs=qo,
        out_shape=jax.ShapeDtypeStruct((batch * seq, ATTN_WIDTH), BF16),
        compiler_params=_cparams(("arbitrary", "arbitrary")),
        name="attention_latent",
    )(attn_sink, q, k, v, cache_k, cache_v)


def _ssd_kernel(xbc_ref, z_ref, dt_ref, init_ref, cw_ref, cb_ref, dtb_ref, alog_ref, dsk_ref, nw_ref,
                tri_ref, exp_ref, y_ref, st_ref, pad_scr, act_scr, sb_scr, sf_scr, sr_scr, *, seq):
    n_chunks = seq // CHUNK
    half = SSD_WIDTH // SSD_GROUPS
    pad_lo = SUBLANES
    pad_scr[0:pad_lo, :] = jnp.zeros((pad_lo, XBC_WIDTH), F32)
    pad_scr[pad_lo:pad_lo + seq, :] = xbc_ref[...]
    pad_scr[pad_lo + seq:2 * pad_lo + seq, :] = jnp.zeros((pad_lo, XBC_WIDTH), F32)

    lane = lax.broadcasted_iota(jnp.int32, (CHUNK, LANES), 1)
    is_fwd = lane < SSD_HEADS
    is_head = lane < 2 * SSD_HEADS
    ti = lax.broadcasted_iota(jnp.int32, (CHUNK, CHUNK), 0)
    tj = lax.broadcasted_iota(jnp.int32, (CHUNK, CHUNK), 1)
    lower = tj <= ti
    upper = tj >= ti
    a_row = jnp.where(is_head[0:1, :], -jnp.exp(alog_ref[...]), 0.0)

    def conv_silu(r0):
        win = pad_scr[pl.ds(r0, CHUNK + 2 * pad_lo), :]
        first = pad_lo - (D_CONV - 1) // 2
        acc = cb_ref[...] + cw_ref[0:1, :] * win[first:first + CHUNK, :]
        for kk in range(1, D_CONV):
            acc = acc + cw_ref[kk:kk + 1, :] * win[first + kk:first + kk + CHUNK, :]
        return acc * _sigmoid(acc)

    def prep(r0):
        x = dt_ref[pl.ds(r0, CHUNK), :] + dtb_ref[...]
        dtv = jnp.maximum(x, 0.0) + jnp.log(1.0 + jnp.exp(-jnp.abs(x)))
        da = dtv * a_row
        hi, mid, lo = _split3(da)
        tri = tri_ref[...]
        cs = _dot(tri, hi) + _dot(tri, mid) + _dot(tri, lo)
        ecs = cs - da
        tot = cs[CHUNK - 1:CHUNK, :]
        ldt = jnp.log(dtv)
        col_q = jnp.where(is_fwd, cs, ecs)
        row_q = jnp.where(is_fwd, cs - ldt, ecs + ldt).T
        scale_y = jnp.where(is_fwd, jnp.exp(cs), jnp.exp(tot - ecs))
        scale_s = dtv * jnp.where(is_fwd, jnp.exp(tot - cs), jnp.exp(ecs))
        ex = exp_ref[...]
        scale_y = _dot(scale_y.astype(BF16), ex)
        scale_s = _dot(scale_s.astype(BF16), ex)
        d_hi, d_mid, d_lo = _split3(jnp.broadcast_to(jnp.exp(tot), (SUBLANES, LANES)))
        decay = (_dot(d_hi, ex) + _dot(d_mid, ex) + _dot(d_lo, ex))[0:1, :]
        return col_q, row_q, scale_y, scale_s, decay

    def chunk_state(act, weights, decay, state):
        xw = (act[:, 0:SSD_WIDTH] * weights).astype(BF16)
        bt = act[:, SSD_WIDTH:SSD_WIDTH + BC_WIDTH].T.astype(BF16)
        upd = jnp.concatenate(
            [_dot(bt[g * D_STATE:(g + 1) * D_STATE, :], xw[:, g * half:(g + 1) * half])
             for g in range(SSD_GROUPS)], axis=-1)
        return decay * state + upd

    sr_scr[...] = init_ref[1]

    def bwd_body(i, carry):
        c = n_chunks - 1 - i
        r0 = pl.multiple_of(c * CHUNK, CHUNK)
        act = conv_silu(r0)
        act_scr[pl.ds(r0, CHUNK), :] = act
        _, _, _, scale_s, decay = prep(r0)
        state = sr_scr[...]
        sb_scr[c] = state
        sr_scr[...] = chunk_state(act, scale_s[:, SSD_WIDTH:], decay[:, SSD_WIDTH:], state)
        return carry

    lax.fori_loop(0, n_chunks, bwd_body, 0)
    st_ref[1] = sr_scr[...]

    sf_scr[...] = init_ref[0]

    def fwd_body(c, carry):
        r0 = pl.multiple_of(c * CHUNK, CHUNK)
        act = act_scr[pl.ds(r0, CHUNK), :]
        col_q, row_q, scale_y, scale_s, decay = prep(r0)
        xs = act[:, 0:SSD_WIDTH]
        xs_b = xs.astype(BF16)
        bmat = act[:, SSD_WIDTH:SSD_WIDTH + BC_WIDTH].astype(BF16)
        cmat = act[:, SSD_WIDTH + BC_WIDTH:XBC_WIDTH].astype(BF16)
        s_f = sf_scr[...]
        s_b = sb_scr[c]
        y_parts = []
        for g in range(SSD_GROUPS):
            gs = slice(g * D_STATE, (g + 1) * D_STATE)
            cb = _dot_nt(cmat[:, gs], bmat[:, gs])
            for pair in range(SSD_HEADS // SSD_GROUPS // 2):
                ws = []
                for h in (g * 4 + 2 * pair, g * 4 + 2 * pair + 1):
                    hb = SSD_HEADS + h
                    e_f = jnp.exp(jnp.where(lower, col_q[:, h:h + 1] - row_q[h:h + 1, :], NEG_INF))
                    e_b = jnp.exp(jnp.where(upper, row_q[hb:hb + 1, :] - col_q[:, hb:hb + 1], NEG_INF))
                    ws.append((cb * (e_f + e_b)).astype(BF16))
                h0 = g * 4 + 2 * pair
                slab = xs_b[:, h0 * SSD_HEADDIM:(h0 + 2) * SSD_HEADDIM]
                first = lane < SSD_HEADDIM
                rhs = jnp.concatenate([jnp.where(first, slab, jnp.zeros_like(slab)),
                                       jnp.where(first, jnp.zeros_like(slab), slab)], axis=0)
                y_parts.append(_dot(jnp.concatenate(ws, axis=-1), rhs))
        y = jnp.concatenate(y_parts, axis=-1)
        off_f = jnp.concatenate(
            [_dot(cmat[:, g * D_STATE:(g + 1) * D_STATE], s_f[:, g * half:(g + 1) * half].astype(BF16))
             for g in range(SSD_GROUPS)], axis=-1)
        off_b = jnp.concatenate(
            [_dot(cmat[:, g * D_STATE:(g + 1) * D_STATE], s_b[:, g * half:(g + 1) * half].astype(BF16))
             for g in range(SSD_GROUPS)], axis=-1)
        y = y + off_f * scale_y[:, 0:SSD_WIDTH] + off_b * scale_y[:, SSD_WIDTH:] + dsk_ref[...] * xs
        zc = z_ref[pl.ds(r0, CHUNK), :]
        gated = y * (zc * _sigmoid(zc))
        outs = []
        for g in range(SSD_GROUPS):
            gg = gated[:, g * half:(g + 1) * half]
            outs.append(gg * lax.rsqrt(jnp.mean(gg * gg, axis=-1, keepdims=True) + EPS))
        y_ref[pl.ds(r0, CHUNK), :] = (jnp.concatenate(outs, axis=-1) * nw_ref[...]).astype(y_ref.dtype)
        sf_scr[...] = chunk_state(act, scale_s[:, 0:SSD_WIDTH], decay[:, 0:SSD_WIDTH], s_f)
        return carry

    lax.fori_loop(0, n_chunks, fwd_body, 0)
    st_ref[0] = sf_scr[...]


def _ssd_constants():
    r = np.arange(CHUNK)
    tri = (r[None, :] <= r[:, None]).astype(np.float32)
    cols = np.arange(2 * SSD_WIDTH)
    expand = (cols[None, :] // SSD_HEADDIM == np.arange(LANES)[:, None]).astype(np.float32)
    return jnp.asarray(tri, BF16), jnp.asarray(expand, BF16)


def _ssd(xbc, z, dt, init, conv_w, conv_b, dt_bias, a_log, d_skip, ssd_norm_w, layer, batch, seq):
    tri, expand = _ssd_constants()
    n_chunks = seq // CHUNK
    row = lambda b: (b, 0)
    lay = lambda b: (layer, 0, 0)
    const = lambda b: (0, 0)
    state_spec = pl.BlockSpec((None, 2, D_STATE, SSD_WIDTH), lambda b: (b, 0, 0, 0))
    return pl.pallas_call(
        functools.partial(_ssd_kernel, seq=seq),
        grid=(batch,),
        in_specs=[
            pl.BlockSpec((seq, XBC_WIDTH), row),
            pl.BlockSpec((seq, SSD_WIDTH), row),
            pl.BlockSpec((seq, LANES), row),
            state_spec,
            pl.BlockSpec((None, SUBLANES, XBC_WIDTH), lay),
            pl.BlockSpec((None, 1, XBC_WIDTH), lay),
            pl.BlockSpec((None, 1, LANES), lay),
            pl.BlockSpec((None, 1, LANES), lay),
            pl.BlockSpec((None, 1, SSD_WIDTH), lay),
            pl.BlockSpec((None, 1, SSD_WIDTH), lay),
            pl.BlockSpec((CHUNK, CHUNK), const),
            pl.BlockSpec((LANES, 2 * SSD_WIDTH), const),
        ],
        out_specs=[pl.BlockSpec((seq, SSD_WIDTH), row), state_spec],
        out_shape=[jax.ShapeDtypeStruct((batch * seq, SSD_WIDTH), BF16),
                   jax.ShapeDtypeStruct((batch, 2, D_STATE, SSD_WIDTH), F32)],
        scratch_shapes=[
            pltpu.VMEM((seq + 2 * SUBLANES, XBC_WIDTH), F32),
            pltpu.VMEM((seq, XBC_WIDTH), F32),
            pltpu.VMEM((n_chunks, D_STATE, SSD_WIDTH), F32),
            pltpu.VMEM((D_STATE, SSD_WIDTH), F32),
            pltpu.VMEM((D_STATE, SSD_WIDTH), F32),
        ],
        compiler_params=_cparams(("arbitrary",)),
        name="ssd",
    )(xbc, z, dt, init, conv_w, conv_b, dt_bias, a_log, d_skip, ssd_norm_w, tri, expand)


def _first_argmax(rows):
    best_v = rows[0]
    best_i = jnp.zeros(rows[0].shape, jnp.int32)
    for i in range(1, len(rows)):
        better = rows[i] > best_v
        best_v = jnp.where(better, rows[i], best_v)
        best_i = jnp.where(better, i, best_i)
    return best_i, best_v


def _outproj_kernel(attn_ref, ssd_ref, x_ref, w_ref, g1_ref, nw_ref, sh_ref, sc_ref, wr_ref, rb_ref,
                    x1_ref, h_ref, gates_ref):
    half = ATTN_WIDTH
    mix = _dot(attn_ref[...], w_ref[0:half, :]) + _dot(ssd_ref[...], w_ref[half:, :])
    x1 = x_ref[...] + g1_ref[...] * mix
    x1_ref[...] = x1
    ms = jnp.mean(x1 * x1, axis=-1, keepdims=True)
    y = x1 * lax.rsqrt(ms + EPS) * nw_ref[...]
    h = y * (1.0 + sc_ref[...]) + sh_ref[...]
    h_hi, h_lo = _split2(h)
    h_ref[...] = h_hi
    wr = wr_ref[...]
    l1 = _dot(h_hi, wr).T
    l2 = _dot(h_lo, wr).T
    ne = N_EXPERTS
    logits = l1[0:ne, :] + l1[ne:2 * ne, :] + l2[0:ne, :]
    scores = _sigmoid(logits)
    sel = scores + rb_ref[...]
    srow = [sel[e:e + 1, :] for e in range(ne)]
    group_score = []
    for g in range(N_EXPERT_GROUPS):
        r = srow[g * EXPERTS_PER_GROUP:(g + 1) * EXPERTS_PER_GROUP]
        pairs = [r[i] + r[j] for i in range(len(r)) for j in range(i + 1, len(r))]
        top2 = pairs[0]
        for p in pairs[1:]:
            top2 = jnp.maximum(top2, p)
        group_score.append(top2)
    best_group, _ = _first_argmax(group_score)
    eid = lax.broadcasted_iota(jnp.int32, sel.shape, 0)
    masked = jnp.where(lax.shift_right_logical(eid, GROUP_SHIFT) == best_group, sel, NEG_INF)
    i1, _ = _first_argmax([masked[e:e + 1, :] for e in range(ne)])
    masked2 = jnp.where(eid == i1, -jnp.inf, masked)
    i2, _ = _first_argmax([masked2[e:e + 1, :] for e in range(ne)])
    pick1 = eid == i1
    pick2 = eid == i2
    w1 = jnp.sum(jnp.where(pick1, scores, 0.0), axis=0, keepdims=True)
    w2 = jnp.sum(jnp.where(pick2, scores, 0.0), axis=0, keepdims=True)
    wsum = w1 + w2
    gates = jnp.where(pick1, w1 / wsum, 0.0) + jnp.where(pick2, w2 / wsum, 0.0)
    pad = jnp.zeros((LANES - ne, gates.shape[1]), F32)
    gates_ref[...] = jnp.concatenate([gates, pad], axis=0).T


def _out_projection(attn, ssd, x, w_out_b, norm_w, mods, wr_p, rb_col, layer, row_of_step):
    t = x.shape[0]
    tm = ROW_TILE
    row = lambda i: (i, 0)
    return pl.pallas_call(
        _outproj_kernel,
        grid=(t // tm,),
        in_specs=[
            pl.BlockSpec((tm, ATTN_WIDTH), row),
            pl.BlockSpec((tm, SSD_WIDTH), row),
            pl.BlockSpec((tm, D_MODEL), row),
            pl.BlockSpec((None, D_MODEL, D_MODEL), lambda i: (layer, 0, 0)),
            _mod_spec(layer, 2, row_of_step),
            pl.BlockSpec((None, 1, D_MODEL), lambda i: (layer, 0, 0)),
            _mod_spec(layer, 3, row_of_step),
            _mod_spec(layer, 4, row_of_step),
            pl.BlockSpec((D_MODEL, LANES), lambda i: (0, 0)),
            pl.BlockSpec((N_EXPERTS, 1), lambda i: (0, 0)),
        ],
        out_specs=[pl.BlockSpec((tm, D_MODEL), row), pl.BlockSpec((tm, D_MODEL), row),
                   pl.BlockSpec((tm, LANES), row)],
        out_shape=[jax.ShapeDtypeStruct((t, D_MODEL), F32), jax.ShapeDtypeStruct((t, D_MODEL), BF16),
                   jax.ShapeDtypeStruct((t, LANES), F32)],
        compiler_params=_cparams(("arbitrary",)),
        name="out_projection",
    )(attn, ssd, x, w_out_b, mods, norm_w, mods, mods, wr_p, rb_col)


def _moe_kernel(h_ref, gates_ref, x1_ref, g2_ref, wg_ref, wu_ref, wd_ref, fw_ref, o_ref, *, final):
    e = pl.program_id(1)

    @pl.when(e == 0)
    def _():
        o_ref[...] = jnp.zeros(o_ref.shape, o_ref.dtype)

    h = h_ref[...]
    gate_act = _dot(h, wg_ref[...].astype(BF16))
    up = _dot(h, wu_ref[...].astype(BF16))
    lane = lax.broadcasted_iota(jnp.int32, gates_ref.shape, 1)
    ge = jnp.sum(jnp.where(lane == e, gates_ref[...], 0.0), axis=-1, keepdims=True)
    hid = gate_act * _sigmoid(gate_act) * up * ge
    o_ref[...] += _dot(hid.astype(BF16), wd_ref[...].astype(BF16))

    @pl.when(e == N_EXPERTS - 1)
    def _():
        x2 = x1_ref[...] + g2_ref[...] * o_ref[...]
        if final:
            ms = jnp.mean(x2 * x2, axis=-1, keepdims=True)
            x2 = x2 * lax.rsqrt(ms + EPS) * fw_ref[...]
        o_ref[...] = x2


def _moe(h, gates, x1, mods, w_gate, w_up, w_down, final_w, layer, row_of_step, final):
    t = h.shape[0]
    tm = MOE_TILE
    row = lambda i, e: (i, 0)
    return pl.pallas_call(
        functools.partial(_moe_kernel, final=final),
        grid=(t // tm, N_EXPERTS),
        in_specs=[
            pl.BlockSpec((tm, D_MODEL), row),
            pl.BlockSpec((tm, LANES), row),
            pl.BlockSpec((tm, D_MODEL), row),
            _mod_spec(layer, 5, lambda i: row_of_step(i)),
            pl.BlockSpec((None, None, D_MODEL, D_FF), lambda i, e: (layer, e, 0, 0)),
            pl.BlockSpec((None, None, D_MODEL, D_FF), lambda i, e: (layer, e, 0, 0)),
            pl.BlockSpec((None, None, D_FF, D_MODEL), lambda i, e: (layer, e, 0, 0)),
            pl.BlockSpec((1, D_MODEL), lambda i, e: (0, 0)),
        ],
        out_specs=pl.BlockSpec((tm, D_MODEL), row),
        out_shape=jax.ShapeDtypeStruct((t, D_MODEL), F32),
        compiler_params=_cparams(("arbitrary", "arbitrary")),
        name="moe_final" if final else "moe",
    )(h, gates, x1, mods, w_gate, w_up, w_down, final_w)


def _rope_tables(n_tokens):
    rows = n_tokens // GRID_W
    row = jnp.broadcast_to(jnp.arange(rows, dtype=F32)[:, None], (rows, GRID_W)).reshape(-1)
    col = jnp.broadcast_to(jnp.arange(GRID_W, dtype=F32)[None, :], (rows, GRID_W)).reshape(-1)
    n_freq = HEAD_DIM // 4
    inv_freq = jnp.power(ROPE_THETA, -jnp.arange(n_freq, dtype=F32) / n_freq)
    ang = jnp.concatenate([row[:, None] * inv_freq, col[:, None] * inv_freq], axis=-1)
    cos, sin = jnp.cos(ang), jnp.sin(ang)
    return jnp.tile(cos, (1, 4)), jnp.concatenate([-sin, sin, -sin, sin], axis=-1)


def kernel(x_prompt, x_sample, cache_k, cache_v, state_ssm, c, c_ctx, norm1_w, norm2_w, final_norm_w,
           w_ada, b_ada, w_in, conv_w, conv_b, attn_sink, dt_bias, a_log, d_skip, ssd_norm_w, w_out,
           w_router, router_bias, w_gate, w_up, w_down):
    bc, sc, d = x_prompt.shape
    bl, sl, _ = x_sample.shape
    depth = w_in.shape[0]
    assert d == D_MODEL and depth == DEPTH and bl + 1 <= MOD_ROWS
    assert sc % ROW_TILE == 0 or ROW_TILE % sc == 0
    assert sl % MOE_TILE == 0 and (bc * sc) % MOE_TILE == 0 and MOE_TILE % sc == 0

    cvec = jnp.zeros((MOD_ROWS, d), F32).at[0].set(c_ctx).at[1:1 + bl].set(c)
    mods = _adaln_all(cvec, w_ada, b_ada).reshape(depth, MOD_ROWS, 6, 1, d)

    scale = HEAD_DIM ** -0.5 * LOG2E
    w_in_p = jnp.concatenate(
        [w_in[:, :, :ATTN_WIDTH] * scale, w_in[:, :, ATTN_WIDTH:],
         jnp.zeros((depth, d, IN_PAD - w_in.shape[2]), F32)], axis=-1).astype(BF16)
    w_out_b = w_out.astype(BF16)
    wr_hi = w_router.astype(BF16)
    wr_lo = (w_router - wr_hi.astype(F32)).astype(BF16)
    wr_p = jnp.concatenate([wr_hi, wr_lo, jnp.zeros((d, LANES - 2 * N_EXPERTS), BF16)], axis=-1)
    rb_col = router_bias.reshape(N_EXPERTS, 1)
    n1 = norm1_w.reshape(depth, 1, d)
    n2 = norm2_w.reshape(depth, 1, d)
    fw = final_norm_w.reshape(1, d)
    conv_w_p = jnp.concatenate([conv_w, jnp.zeros((depth, SUBLANES - D_CONV, XBC_WIDTH), F32)], axis=1)
    conv_b_p = conv_b.reshape(depth, 1, XBC_WIDTH)
    pad16 = lambda a: jnp.concatenate(
        [a.reshape(depth, 1, 2 * SSD_HEADS), jnp.zeros((depth, 1, LANES - 2 * SSD_HEADS), F32)], axis=-1)
    dtb_p = pad16(dt_bias)
    alog_p = pad16(a_log)
    dsk_p = jnp.repeat(d_skip, SSD_HEADDIM, axis=-1).reshape(depth, 1, SSD_WIDTH)
    snw_p = ssd_norm_w.reshape(depth, 1, SSD_WIDTH)
    rope_tabs = _rope_tables(sl)
    ck = cache_k.reshape(bl, depth, cache_k.shape[2], KV_WIDTH)
    cv = cache_v.reshape(bl, depth, cache_v.shape[2], KV_WIDTH)
    st_in = jnp.transpose(state_ssm, (0, 1, 2, 5, 3, 4)).reshape(bl, depth, 2, D_STATE, SSD_WIDTH)
    zero_state = jnp.zeros((bc, 2, D_STATE, SSD_WIDTH), F32)

    ctx_row = lambda i: 0
    lat_row_proj = lambda i: 1 + i // (sl // ROW_TILE)
    lat_row_moe = lambda i: 1 + i // (sl // MOE_TILE)

    xp = x_prompt.reshape(bc * sc, d)
    xs = x_sample.reshape(bl * sl, d)
    new_k, new_v, new_s = [], [], []
    for l in range(depth):
        last = l == depth - 1
        q, k, v, xbc, z, dt = _in_projection(xp, n1, mods, w_in_p, l, ctx_row, None, sc)
        attn = _attention_context(q, k, v, attn_sink, l, bc, sc)
        ssd, s_ctx = _ssd(xbc, z, dt, zero_state, conv_w_p, conv_b_p, dtb_p, alog_p, dsk_p, snw_p, l, bc, sc)
        x1, h2, gates = _out_projection(attn, ssd, xp, w_out_b, n2, mods, wr_p, rb_col, l, ctx_row)
        xp = _moe(h2, gates, x1, mods, w_gate, w_up, w_down, fw, l, ctx_row, last)
        new_k.append(k.reshape(bc, sc, N_KV_HEADS, HEAD_DIM))
        new_v.append(v.reshape(bc, sc, N_KV_HEADS, HEAD_DIM))
        new_s.append(jnp.transpose(s_ctx.reshape(bc, 2, D_STATE, SSD_HEADS, SSD_HEADDIM), (0, 1, 3, 4, 2)))
        q, k, v, xbc, z, dt = _in_projection(xs, n1, mods, w_in_p, l, lat_row_proj, rope_tabs, sl)
        attn = _attention_latent(q, k, v, ck, cv, attn_sink, l, bl, sl)
        ssd, _ = _ssd(xbc, z, dt, st_in[:, l], conv_w_p, conv_b_p, dtb_p, alog_p, dsk_p, snw_p, l, bl, sl)
        x1, h2, gates = _out_projection(attn, ssd, xs, w_out_b, n2, mods, wr_p, rb_col, l, lat_row_proj)
        xs = _moe(h2, gates, x1, mods, w_gate, w_up, w_down, fw, l, lat_row_moe, last)
    return (xp.reshape(bc, sc, d), xs.reshape(bl, sl, d),
            jnp.stack(new_k, axis=1), jnp.stack(new_v, axis=1), jnp.stack(new_s, axis=1))
```

```python
import functools

import numpy as np
import jax
import jax.numpy as jnp
from jax import lax
from jax.experimental import pallas as pl
from jax.experimental.pallas import tpu as pltpu

F32 = jnp.float32
BF16 = jnp.bfloat16

D_MODEL = 1024
DEPTH = 4
GRID_W = 64
EPS = 1e-6
NEG_INF = -1e30
N_HEADS = 8
N_KV_HEADS = 2
GQA_GROUP = N_HEADS // N_KV_HEADS
HEAD_DIM = 64
ATTN_WIDTH = N_HEADS * HEAD_DIM
KV_WIDTH = N_KV_HEADS * HEAD_DIM
BLOCK = 128
ROPE_THETA = 10000.0
SSD_HEADS = 8
SSD_HEADDIM = 64
SSD_WIDTH = SSD_HEADS * SSD_HEADDIM
SSD_GROUPS = 2
D_STATE = 64
BC_WIDTH = SSD_GROUPS * D_STATE
D_CONV = 5
CHUNK = 128
XBC_WIDTH = SSD_WIDTH + 2 * BC_WIDTH
N_EXPERTS = 16
N_EXPERT_GROUPS = 4
EXPERTS_PER_GROUP = N_EXPERTS // N_EXPERT_GROUPS
D_FF = 256
GROUP_SHIFT = EXPERTS_PER_GROUP.bit_length() - 1
assert 1 << GROUP_SHIFT == EXPERTS_PER_GROUP

LANES = 128
SUBLANES = 8
VMEM_LIMIT = 56 * 1024 * 1024

O_Q = 0
O_K = O_Q + ATTN_WIDTH
O_V = O_K + KV_WIDTH
O_XBC = O_V + KV_WIDTH
O_Z = O_XBC + XBC_WIDTH
O_DT = O_Z + SSD_WIDTH
IN_PAD = O_DT + LANES
MOD_ROWS = 16
ROW_TILE = 512
MOE_TILE = 512
MOE_BLOCK = 128
GROUP_LANE = N_EXPERTS
GROUP_COUNT_SHIFT = N_EXPERT_GROUPS.bit_length() - 1
assert 1 << GROUP_COUNT_SHIFT == N_EXPERT_GROUPS
SSD_UNROLL = 4
LAT_QBLOCKS = 4
LOG2E = 1.4426950408889634


def _cparams(sem):
    return pltpu.CompilerParams(dimension_semantics=sem, vmem_limit_bytes=VMEM_LIMIT)


def _sigmoid(x):
    return 1.0 / (1.0 + jnp.exp(-x))


def _split2(x):
    hi = x.astype(BF16)
    lo = (x - hi.astype(F32)).astype(BF16)
    return hi, lo


def _split3(x):
    hi = x.astype(BF16)
    r = x - hi.astype(F32)
    mid = r.astype(BF16)
    lo = (r - mid.astype(F32)).astype(BF16)
    return hi, mid, lo


def _dot(a, b):
    return jnp.dot(a, b, preferred_element_type=F32)


def _dot_nt(a, b):
    return lax.dot_general(a, b, (((1,), (1,)), ((), ())), preferred_element_type=F32)


def _adaln_kernel(c_ref, w_ref, b_ref, o_ref):
    c = c_ref[...]
    s = c * _sigmoid(c)
    s_hi, s_lo = _split2(s)
    w = w_ref[...]
    w_hi, w_lo = _split2(w)
    acc = _dot(s_hi, w_hi) + _dot(s_lo, w_hi) + _dot(s_hi, w_lo)
    o_ref[...] = acc + b_ref[...]


def _adaln_all(cvec, w_ada, b_ada):
    depth, d, n = w_ada.shape
    tn = 1024
    return pl.pallas_call(
        _adaln_kernel,
        grid=(depth, n // tn),
        in_specs=[
            pl.BlockSpec((MOD_ROWS, d), lambda l, j: (0, 0)),
            pl.BlockSpec((None, d, tn), lambda l, j: (l, 0, j)),
            pl.BlockSpec((None, 1, tn), lambda l, j: (l, 0, j)),
        ],
        out_specs=pl.BlockSpec((None, MOD_ROWS, tn), lambda l, j: (l, 0, j)),
        out_shape=jax.ShapeDtypeStruct((depth, MOD_ROWS, n), F32),
        compiler_params=_cparams(("arbitrary", "arbitrary")),
        name="adaln",
    )(cvec, w_ada, b_ada.reshape(depth, 1, n))


def _mod_spec(layer, which, row_of_step):
    return pl.BlockSpec((None, None, None, 1, D_MODEL),
                        lambda i, *_: (layer, row_of_step(i), which, 0, 0))


def _rope(x, cos4, sin4):
    lane = lax.broadcasted_iota(jnp.int32, x.shape, 1)
    first_half = (lane & (HEAD_DIM - 1)) < (HEAD_DIM // 2)
    partner = jnp.where(first_half,
                        pltpu.roll(x, LANES - HEAD_DIM // 2, 1),
                        pltpu.roll(x, HEAD_DIM // 2, 1))
    return x * cos4 + partner * sin4


def _inproj_kernel(*refs, rope):
    if rope:
        (x_ref, nw_ref, sh_ref, sc_ref, w_ref, cos_ref, sin_ref,
         q_ref, k_ref, v_ref, xbc_ref, z_ref, dt_ref) = refs
    else:
        (x_ref, nw_ref, sh_ref, sc_ref, w_ref,
         q_ref, k_ref, v_ref, xbc_ref, z_ref, dt_ref) = refs
    x = x_ref[...]
    ms = jnp.mean(x * x, axis=-1, keepdims=True)
    y = x * lax.rsqrt(ms + EPS) * nw_ref[...]
    h = (y * (1.0 + sc_ref[...]) + sh_ref[...]).astype(BF16)

    def seg(a, b):
        return _dot(h, w_ref[:, a:b])

    if rope:
        cos4 = cos_ref[...]
        sin4 = sin_ref[...]
        for j in range(ATTN_WIDTH // LANES):
            q_ref[:, j * LANES:(j + 1) * LANES] = _rope(
                seg(O_Q + j * LANES, O_Q + (j + 1) * LANES), cos4, sin4).astype(q_ref.dtype)
        k_ref[...] = _rope(seg(O_K, O_V), cos4, sin4)
    else:
        q_ref[...] = seg(O_Q, O_K).astype(q_ref.dtype)
        k_ref[...] = seg(O_K, O_V)
    v_ref[...] = seg(O_V, O_XBC)
    xbc_ref[...] = seg(O_XBC, O_Z)
    z_ref[...] = seg(O_Z, O_DT)
    dt_ref[...] = seg(O_DT, IN_PAD)


def _in_projection(x, norm_w, mods, w_in_p, layer, row_of_step, rope_tabs, seq_len):
    t = x.shape[0]
    tm = ROW_TILE
    rope = rope_tabs is not None
    row = lambda i: (i, 0)
    in_specs = [
        pl.BlockSpec((tm, D_MODEL), row),
        pl.BlockSpec((None, 1, D_MODEL), lambda i: (layer, 0, 0)),
        _mod_spec(layer, 0, row_of_step),
        _mod_spec(layer, 1, row_of_step),
        pl.BlockSpec((None, D_MODEL, IN_PAD), lambda i: (layer, 0, 0)),
    ]
    args = [x, norm_w, mods, mods, w_in_p]
    if rope:
        steps_per_seq = seq_len // tm
        tab = pl.BlockSpec((tm, LANES), lambda i: (i % steps_per_seq, 0))
        in_specs += [tab, tab]
        args += list(rope_tabs)
    widths = (ATTN_WIDTH, KV_WIDTH, KV_WIDTH, XBC_WIDTH, SSD_WIDTH, LANES)
    dtypes = (BF16, F32, F32, F32, F32, F32)
    return pl.pallas_call(
        functools.partial(_inproj_kernel, rope=rope),
        grid=(t // tm,),
        in_specs=in_specs,
        out_specs=[pl.BlockSpec((tm, w), row) for w in widths],
        out_shape=[jax.ShapeDtypeStruct((t, w), dt) for w, dt in zip(widths, dtypes)],
        compiler_params=_cparams(("arbitrary",)),
        name="in_projection_rope" if rope else "in_projection",
    )(*args)


def _kv_operands(k, v):
    low = lax.broadcasted_iota(jnp.int32, k.shape, 1) < HEAD_DIM
    k_sw = pltpu.roll(k, HEAD_DIM, 1)
    v_sw = pltpu.roll(v, HEAD_DIM, 1)
    kdup = (jnp.where(low, k, k_sw).astype(BF16), jnp.where(low, k_sw, k).astype(BF16))
    vdup = (jnp.where(low, v, v_sw).astype(BF16), jnp.where(low, v_sw, v).astype(BF16))
    return kdup, vdup


def _attend_group(sink_ref, q_ref, o_ref, layer, g, r0, tq, segments):
    low = lax.broadcasted_iota(jnp.int32, (tq, LANES), 1) < HEAD_DIM
    base = g * GQA_GROUP * HEAD_DIM
    rows = []
    for j in range(GQA_GROUP // 2):
        slab = q_ref[r0:r0 + tq, base + j * LANES:base + (j + 1) * LANES]
        zero = jnp.zeros_like(slab)
        rows += [jnp.where(low, slab, zero), jnp.where(low, zero, slab)]
    lhs = jnp.concatenate(rows, axis=0)
    sink = jnp.concatenate(
        [jnp.full((tq, LANES), sink_ref[layer, g * GQA_GROUP + i] * LOG2E, F32) for i in range(GQA_GROUP)],
        axis=0)
    tiles = []
    for kdup, _, patches in segments:
        s = _dot_nt(lhs, kdup)
        ts = [s[:, c * LANES:(c + 1) * LANES] for c in range(s.shape[1] // LANES)]
        for c, mask in patches or ():
            ts[c] = (ts[c].reshape(GQA_GROUP, tq, LANES) + mask[None]).reshape(ts[c].shape)
        tiles.append(ts)
    flat = [t for ts in tiles for t in ts]
    m = jnp.max(functools.reduce(jnp.maximum, flat + [sink]), axis=-1, keepdims=True)
    probs = [[jnp.exp2(t - m) for t in ts] for ts in tiles]
    total = functools.reduce(jnp.add, [p for ps in probs for p in ps])
    lane0 = lax.broadcasted_iota(jnp.int32, sink.shape, 1) == 0
    total = total + jnp.where(lane0, jnp.exp2(sink - m), 0.0)
    denom = jnp.sum(total, axis=-1, keepdims=True)
    o = None
    for ps, (_, vdup, _) in zip(probs, segments):
        part = _dot(jnp.concatenate(ps, axis=-1).astype(BF16), vdup)
        o = part if o is None else o + part
    o = o * (1.0 / denom)
    for j in range(GQA_GROUP // 2):
        pair = jnp.where(low, o[2 * j * tq:(2 * j + 1) * tq], o[(2 * j + 1) * tq:(2 * j + 2) * tq])
        o_ref[r0:r0 + tq, base + j * LANES:base + (j + 1) * LANES] = pair.astype(o_ref.dtype)


def _attn_ctx_kernel(sink_ref, q_ref, k_ref, v_ref, o_ref, *, layer):
    kdup, vdup = _kv_operands(k_ref[...], v_ref[...])
    for x in range(q_ref.shape[0] // BLOCK):
        for g in range(N_KV_HEADS):
            _attend_group(sink_ref, q_ref, o_ref, layer, g, x * BLOCK, BLOCK, [(kdup[g], vdup[g], None)])


def _attention_context(q, k, v, attn_sink, layer, batch, seq):
    row = lambda b: (b, 0)
    return pl.pallas_call(
        functools.partial(_attn_ctx_kernel, layer=layer),
        grid=(batch,),
        in_specs=[
            pl.BlockSpec(memory_space=pltpu.SMEM),
            pl.BlockSpec((seq, ATTN_WIDTH), row),
            pl.BlockSpec((seq, KV_WIDTH), row),
            pl.BlockSpec((seq, KV_WIDTH), row),
        ],
        out_specs=pl.BlockSpec((seq, ATTN_WIDTH), row),
        out_shape=jax.ShapeDtypeStruct((batch * seq, ATTN_WIDTH), BF16),
        compiler_params=_cparams(("arbitrary",)),
        name="attention_context",
    )(attn_sink, q, k, v)


def _attn_lat_kernel(sink_ref, q_ref, k_ref, v_ref, kc_ref, vc_ref, o_ref, *, layer, n_blocks):
    first_block = pl.program_id(1) * LAT_QBLOCKS
    band_blocks = LAT_QBLOCKS + 2

    def band(seq_ref):
        parts = []
        for i in range(band_blocks):
            blk = jnp.clip(first_block - 1 + i, 0, n_blocks - 1)
            parts.append(seq_ref[pl.ds(pl.multiple_of(blk * BLOCK, BLOCK), BLOCK), :])
        return jnp.concatenate(parts, axis=0)

    past_tiles = kc_ref.shape[0] // BLOCK
    kdup, vdup = _kv_operands(jnp.concatenate([kc_ref[...], band(k_ref)], axis=0),
                              jnp.concatenate([vc_ref[...], band(v_ref)], axis=0))
    qi = lax.broadcasted_iota(jnp.int32, (BLOCK, BLOCK), 0)
    kj = lax.broadcasted_iota(jnp.int32, (BLOCK, BLOCK), 1)
    for x in range(LAT_QBLOCKS):
        blk = first_block + x
        mask_prev = jnp.where(kj >= qi + jnp.where(blk > 0, 0, BLOCK), 0.0, NEG_INF)
        mask_next = jnp.where(kj <= qi - jnp.where(blk < n_blocks - 1, 0, BLOCK), 0.0, NEG_INF)
        patches = [(past_tiles, mask_prev), (past_tiles + 2, mask_next)]
        for g in range(N_KV_HEADS):
            if LAT_QBLOCKS == 1:
                keys, vals = kdup[g], vdup[g]
            else:
                pick = lambda a: jnp.concatenate(
                    [a[:past_tiles * BLOCK], a[(past_tiles + x) * BLOCK:(past_tiles + x + 3) * BLOCK]], axis=0)
                keys, vals = pick(kdup[g]), pick(vdup[g])
            _attend_group(sink_ref, q_ref, o_ref, layer, g, x * BLOCK, BLOCK, [(keys, vals, patches)])


def _attention_latent(q, k, v, cache_k, cache_v, attn_sink, layer, batch, seq):
    n_blocks = seq // BLOCK
    steps = n_blocks // LAT_QBLOCKS
    tq = LAT_QBLOCKS * BLOCK
    past = cache_k.shape[2]
    ctx = pl.BlockSpec((None, None, past, KV_WIDTH), lambda b, n: (b, layer, 0, 0))
    whole = pl.BlockSpec((seq, KV_WIDTH), lambda b, n: (b, 0))
    qo = pl.BlockSpec((tq, ATTN_WIDTH), lambda b, n: (b * steps + n, 0))
    return pl.pallas_call(
        functools.partial(_attn_lat_kernel, layer=layer, n_blocks=n_blocks),
        grid=(batch, steps),
        in_specs=[pl.BlockSpec(memory_space=pltpu.SMEM), qo, whole, whole, ctx, ctx],
        out_specs=qo,
        out_shape=jax.ShapeDtypeStruct((batch * seq, ATTN_WIDTH), BF16),
        compiler_params=_cparams(("arbitrary", "arbitrary")),
        name="attention_latent",
    )(attn_sink, q, k, v, cache_k, cache_v)


def _ssd_kernel(xbc_ref, z_ref, dt_ref, init_ref, cw_ref, cb_ref, dtb_ref, alog_ref, dsk_ref, nw_ref,
                tri_ref, exp_ref, y_ref, st_ref, pad_scr, act_scr, sb_scr, sf_scr, sr_scr, *, seq):
    n_chunks = seq // CHUNK
    half = SSD_WIDTH // SSD_GROUPS
    pad_lo = SUBLANES
    pad_scr[0:pad_lo, :] = jnp.zeros((pad_lo, XBC_WIDTH), F32)
    pad_scr[pad_lo:pad_lo + seq, :] = xbc_ref[...]
    pad_scr[pad_lo + seq:2 * pad_lo + seq, :] = jnp.zeros((pad_lo, XBC_WIDTH), F32)

    lane = lax.broadcasted_iota(jnp.int32, (CHUNK, LANES), 1)
    is_fwd = lane < SSD_HEADS
    is_head = lane < 2 * SSD_HEADS
    ti = lax.broadcasted_iota(jnp.int32, (CHUNK, CHUNK), 0)
    tj = lax.broadcasted_iota(jnp.int32, (CHUNK, CHUNK), 1)
    lower = tj <= ti
    upper = tj >= ti
    a_row = jnp.where(is_head[0:1, :], -jnp.exp(alog_ref[...]), 0.0)

    def conv_silu(r0):
        win = pad_scr[pl.ds(r0, CHUNK + 2 * pad_lo), :]
        first = pad_lo - (D_CONV - 1) // 2
        acc = cb_ref[...] + cw_ref[0:1, :] * win[first:first + CHUNK, :]
        for kk in range(1, D_CONV):
            acc = acc + cw_ref[kk:kk + 1, :] * win[first + kk:first + kk + CHUNK, :]
        return acc * _sigmoid(acc)

    def prep(r0):
        x = dt_ref[pl.ds(r0, CHUNK), :] + dtb_ref[...]
        dtv = jnp.maximum(x, 0.0) + jnp.log(1.0 + jnp.exp(-jnp.abs(x)))
        da = dtv * a_row
        hi, mid, lo = _split3(da)
        tri = tri_ref[...]
        cs = _dot(tri, hi) + _dot(tri, mid) + _dot(tri, lo)
        ecs = cs - da
        tot = cs[CHUNK - 1:CHUNK, :]
        ldt = jnp.log(dtv)
        col_q = jnp.where(is_fwd, cs, ecs)
        row_q = jnp.where(is_fwd, cs - ldt, ecs + ldt).T
        scale_y = jnp.where(is_fwd, jnp.exp(cs), jnp.exp(tot - ecs))
        scale_s = dtv * jnp.where(is_fwd, jnp.exp(tot - cs), jnp.exp(ecs))
        ex = exp_ref[...]
        scale_y = _dot(scale_y.astype(BF16), ex)
        scale_s = _dot(scale_s.astype(BF16), ex)
        d_hi, d_mid, d_lo = _split3(jnp.broadcast_to(jnp.exp(tot), (SUBLANES, LANES)))
        decay = (_dot(d_hi, ex) + _dot(d_mid, ex) + _dot(d_lo, ex))[0:1, :]
        return col_q, row_q, scale_y, scale_s, decay

    def chunk_state(act, weights, decay, state):
        xw = (act[:, 0:SSD_WIDTH] * weights).astype(BF16)
        bt = act[:, SSD_WIDTH:SSD_WIDTH + BC_WIDTH].T.astype(BF16)
        upd = jnp.concatenate(
            [_dot(bt[g * D_STATE:(g + 1) * D_STATE, :], xw[:, g * half:(g + 1) * half])
             for g in range(SSD_GROUPS)], axis=-1)
        return decay * state + upd

    sr_scr[...] = init_ref[1]

    def bwd_body(i, carry):
        c = n_chunks - 1 - i
        r0 = pl.multiple_of(c * CHUNK, CHUNK)
        act = conv_silu(r0)
        act_scr[pl.ds(r0, CHUNK), :] = act
        _, _, _, scale_s, decay = prep(r0)
        state = sr_scr[...]
        sb_scr[c] = state
        sr_scr[...] = chunk_state(act, scale_s[:, SSD_WIDTH:], decay[:, SSD_WIDTH:], state)
        return carry

    lax.fori_loop(0, n_chunks, bwd_body, 0, unroll=min(SSD_UNROLL, n_chunks))
    st_ref[1] = sr_scr[...]

    sf_scr[...] = init_ref[0]

    def fwd_body(c, carry):
        r0 = pl.multiple_of(c * CHUNK, CHUNK)
        act = act_scr[pl.ds(r0, CHUNK), :]
        col_q, row_q, scale_y, scale_s, decay = prep(r0)
        xs = act[:, 0:SSD_WIDTH]
        xs_b = xs.astype(BF16)
        bmat = act[:, SSD_WIDTH:SSD_WIDTH + BC_WIDTH].astype(BF16)
        cmat = act[:, SSD_WIDTH + BC_WIDTH:XBC_WIDTH].astype(BF16)
        s_f = sf_scr[...]
        s_b = sb_scr[c]
        y_parts = []
        for g in range(SSD_GROUPS):
            gs = slice(g * D_STATE, (g + 1) * D_STATE)
            cb = _dot_nt(cmat[:, gs], bmat[:, gs])
            for pair in range(SSD_HEADS // SSD_GROUPS // 2):
                ws = []
                for h in (g * 4 + 2 * pair, g * 4 + 2 * pair + 1):
                    hb = SSD_HEADS + h
                    e_f = jnp.exp(jnp.where(lower, col_q[:, h:h + 1] - row_q[h:h + 1, :], NEG_INF))
                    e_b = jnp.exp(jnp.where(upper, row_q[hb:hb + 1, :] - col_q[:, hb:hb + 1], NEG_INF))
                    ws.append((cb * (e_f + e_b)).astype(BF16))
                h0 = g * 4 + 2 * pair
                slab = xs_b[:, h0 * SSD_HEADDIM:(h0 + 2) * SSD_HEADDIM]
                first = lane < SSD_HEADDIM
                rhs = jnp.concatenate([jnp.where(first, slab, jnp.zeros_like(slab)),
                                       jnp.where(first, jnp.zeros_like(slab), slab)], axis=0)
                y_parts.append(_dot(jnp.concatenate(ws, axis=-1), rhs))
        y = jnp.concatenate(y_parts, axis=-1)
        off_f = jnp.concatenate(
            [_dot(cmat[:, g * D_STATE:(g + 1) * D_STATE], s_f[:, g * half:(g + 1) * half].astype(BF16))
             for g in range(SSD_GROUPS)], axis=-1)
        off_b = jnp.concatenate(
            [_dot(cmat[:, g * D_STATE:(g + 1) * D_STATE], s_b[:, g * half:(g + 1) * half].astype(BF16))
             for g in range(SSD_GROUPS)], axis=-1)
        y = y + off_f * scale_y[:, 0:SSD_WIDTH] + off_b * scale_y[:, SSD_WIDTH:] + dsk_ref[...] * xs
        zc = z_ref[pl.ds(r0, CHUNK), :]
        gated = y * (zc * _sigmoid(zc))
        outs = []
        for g in range(SSD_GROUPS):
            gg = gated[:, g * half:(g + 1) * half]
            outs.append(gg * lax.rsqrt(jnp.mean(gg * gg, axis=-1, keepdims=True) + EPS))
        y_ref[pl.ds(r0, CHUNK), :] = (jnp.concatenate(outs, axis=-1) * nw_ref[...]).astype(y_ref.dtype)
        sf_scr[...] = chunk_state(act, scale_s[:, 0:SSD_WIDTH], decay[:, 0:SSD_WIDTH], s_f)
        return carry

    lax.fori_loop(0, n_chunks, fwd_body, 0, unroll=min(SSD_UNROLL, n_chunks))
    st_ref[0] = sf_scr[...]


def _ssd_constants():
    r = np.arange(CHUNK)
    tri = (r[None, :] <= r[:, None]).astype(np.float32)
    cols = np.arange(2 * SSD_WIDTH)
    expand = (cols[None, :] // SSD_HEADDIM == np.arange(LANES)[:, None]).astype(np.float32)
    return jnp.asarray(tri, BF16), jnp.asarray(expand, BF16)


def _ssd(xbc, z, dt, init, conv_w, conv_b, dt_bias, a_log, d_skip, ssd_norm_w, layer, batch, seq):
    tri, expand = _ssd_constants()
    n_chunks = seq // CHUNK
    row = lambda b: (b, 0)
    lay = lambda b: (layer, 0, 0)
    const = lambda b: (0, 0)
    state_spec = pl.BlockSpec((None, 2, D_STATE, SSD_WIDTH), lambda b: (b, 0, 0, 0))
    return pl.pallas_call(
        functools.partial(_ssd_kernel, seq=seq),
        grid=(batch,),
        in_specs=[
            pl.BlockSpec((seq, XBC_WIDTH), row),
            pl.BlockSpec((seq, SSD_WIDTH), row),
            pl.BlockSpec((seq, LANES), row),
            state_spec,
            pl.BlockSpec((None, SUBLANES, XBC_WIDTH), lay),
            pl.BlockSpec((None, 1, XBC_WIDTH), lay),
            pl.BlockSpec((None, 1, LANES), lay),
            pl.BlockSpec((None, 1, LANES), lay),
            pl.BlockSpec((None, 1, SSD_WIDTH), lay),
            pl.BlockSpec((None, 1, SSD_WIDTH), lay),
            pl.BlockSpec((CHUNK, CHUNK), const),
            pl.BlockSpec((LANES, 2 * SSD_WIDTH), const),
        ],
        out_specs=[pl.BlockSpec((seq, SSD_WIDTH), row), state_spec],
        out_shape=[jax.ShapeDtypeStruct((batch * seq, SSD_WIDTH), BF16),
                   jax.ShapeDtypeStruct((batch, 2, D_STATE, SSD_WIDTH), F32)],
        scratch_shapes=[
            pltpu.VMEM((seq + 2 * SUBLANES, XBC_WIDTH), F32),
            pltpu.VMEM((seq, XBC_WIDTH), F32),
            pltpu.VMEM((n_chunks, D_STATE, SSD_WIDTH), F32),
            pltpu.VMEM((D_STATE, SSD_WIDTH), F32),
            pltpu.VMEM((D_STATE, SSD_WIDTH), F32),
        ],
        compiler_params=_cparams(("arbitrary",)),
        name="ssd",
    )(xbc, z, dt, init, conv_w, conv_b, dt_bias, a_log, d_skip, ssd_norm_w, tri, expand)


def _first_argmax(rows):
    best_v = rows[0]
    best_i = jnp.zeros(rows[0].shape, jnp.int32)
    for i in range(1, len(rows)):
        better = rows[i] > best_v
        best_v = jnp.where(better, rows[i], best_v)
        best_i = jnp.where(better, i, best_i)
    return best_i, best_v


def _outproj_kernel(attn_ref, ssd_ref, x_ref, w_ref, g1_ref, nw_ref, sh_ref, sc_ref, wr_ref, rb_ref,
                    x1_ref, h_ref, gates_ref):
    half = ATTN_WIDTH
    mix = _dot(attn_ref[...], w_ref[0:half, :]) + _dot(ssd_ref[...], w_ref[half:, :])
    x1 = x_ref[...] + g1_ref[...] * mix
    x1_ref[...] = x1
    ms = jnp.mean(x1 * x1, axis=-1, keepdims=True)
    y = x1 * lax.rsqrt(ms + EPS) * nw_ref[...]
    h = y * (1.0 + sc_ref[...]) + sh_ref[...]
    h_hi, h_lo = _split2(h)
    h_ref[...] = h_hi
    wr = wr_ref[...]
    l1 = _dot(h_hi, wr).T
    l2 = _dot(h_lo, wr).T
    ne = N_EXPERTS
    logits = l1[0:ne, :] + l1[ne:2 * ne, :] + l2[0:ne, :]
    scores = _sigmoid(logits)
    sel = scores + rb_ref[...]
    srow = [sel[e:e + 1, :] for e in range(ne)]
    group_score = []
    for g in range(N_EXPERT_GROUPS):
        r = srow[g * EXPERTS_PER_GROUP:(g + 1) * EXPERTS_PER_GROUP]
        pairs = [r[i] + r[j] for i in range(len(r)) for j in range(i + 1, len(r))]
        top2 = pairs[0]
        for p in pairs[1:]:
            top2 = jnp.maximum(top2, p)
        group_score.append(top2)
    best_group, _ = _first_argmax(group_score)
    eid = lax.broadcasted_iota(jnp.int32, sel.shape, 0)
    masked = jnp.where(lax.shift_right_logical(eid, GROUP_SHIFT) == best_group, sel, NEG_INF)
    i1, _ = _first_argmax([masked[e:e + 1, :] for e in range(ne)])
    masked2 = jnp.where(eid == i1, -jnp.inf, masked)
    i2, _ = _first_argmax([masked2[e:e + 1, :] for e in range(ne)])
    pick1 = eid == i1
    pick2 = eid == i2
    w1 = jnp.sum(jnp.where(pick1, scores, 0.0), axis=0, keepdims=True)
    w2 = jnp.sum(jnp.where(pick2, scores, 0.0), axis=0, keepdims=True)
    wsum = w1 + w2
    gates = jnp.where(pick1, w1 / wsum, 0.0) + jnp.where(pick2, w2 / wsum, 0.0)
    gid = lax.broadcasted_iota(jnp.int32, (N_EXPERT_GROUPS, gates.shape[1]), 0)
    onehot = jnp.where(gid == best_group, 1.0, 0.0)
    pad = jnp.zeros((LANES - ne - N_EXPERT_GROUPS, gates.shape[1]), F32)
    gates_ref[...] = jnp.concatenate([gates, onehot, pad], axis=0).T


def _out_projection(attn, ssd, x, w_out_b, norm_w, mods, wr_p, rb_col, layer, row_of_step):
    t = x.shape[0]
    tm = ROW_TILE
    row = lambda i: (i, 0)
    return pl.pallas_call(
        _outproj_kernel,
        grid=(t // tm,),
        in_specs=[
            pl.BlockSpec((tm, ATTN_WIDTH), row),
            pl.BlockSpec((tm, SSD_WIDTH), row),
            pl.BlockSpec((tm, D_MODEL), row),
            pl.BlockSpec((None, D_MODEL, D_MODEL), lambda i: (layer, 0, 0)),
            _mod_spec(layer, 2, row_of_step),
            pl.BlockSpec((None, 1, D_MODEL), lambda i: (layer, 0, 0)),
            _mod_spec(layer, 3, row_of_step),
            _mod_spec(layer, 4, row_of_step),
            pl.BlockSpec((D_MODEL, LANES), lambda i: (0, 0)),
            pl.BlockSpec((N_EXPERTS, 1), lambda i: (0, 0)),
        ],
        out_specs=[pl.BlockSpec((tm, D_MODEL), row), pl.BlockSpec((tm, D_MODEL), row),
                   pl.BlockSpec((tm, LANES), row)],
        out_shape=[jax.ShapeDtypeStruct((t, D_MODEL), F32), jax.ShapeDtypeStruct((t, D_MODEL), BF16),
                   jax.ShapeDtypeStruct((t, LANES), F32)],
        compiler_params=_cparams(("arbitrary",)),
        name="out_projection",
    )(attn, ssd, x, w_out_b, mods, norm_w, mods, mods, wr_p, rb_col)


def _moe_kernel(h_ref, gates_ref, x1_ref, g2_ref, wg_ref, wu_ref, wd_ref, fw_ref, tril_ref, triu_ref,
                o_ref, xs_scr, gs_scr, acc_scr, seg_ref, *, final):
    tm = h_ref.shape[0]
    g = gates_ref[...]
    lane = lax.broadcasted_iota(jnp.int32, g.shape, 1)
    onehot = jnp.where((lane >= GROUP_LANE) & (lane < GROUP_LANE + N_EXPERT_GROUPS), g, 0.0)
    cnt = _dot(tril_ref[...], onehot.astype(BF16))
    tot = cnt[tm - 1:tm, :]
    lane1 = lax.broadcasted_iota(jnp.int32, tot.shape, 1)
    offs = [jnp.sum(jnp.where((lane1 >= GROUP_LANE) & (lane1 < GROUP_LANE + k), tot, 0.0),
                    axis=-1, keepdims=True) for k in range(N_EXPERT_GROUPS)]
    off_row = functools.reduce(jnp.add, [jnp.where(lane1 == GROUP_LANE + k, offs[k], 0.0)
                                         for k in range(N_EXPERT_GROUPS)])
    pos_col = jnp.sum(onehot * (cnt + off_row - 1.0), axis=-1, keepdims=True)
    r_lane = lax.broadcasted_iota(jnp.int32, (tm, tm), 1).astype(F32)
    unsort = jnp.where(r_lane == pos_col, 1.0, 0.0).astype(BF16)
    onehot_t = g.T[GROUP_LANE:GROUP_LANE + SUBLANES, :]
    sub = lax.broadcasted_iota(jnp.int32, onehot_t.shape, 0)
    onehot_t = jnp.where(sub < N_EXPERT_GROUPS, onehot_t, 0.0)
    cnt_t = _dot(onehot_t.astype(BF16), triu_ref[...])
    off_t = functools.reduce(jnp.add, [jnp.where(sub == k, offs[k], 0.0) for k in range(N_EXPERT_GROUPS)])
    pos_row = jnp.sum(onehot_t * (cnt_t + off_t - 1.0), axis=0, keepdims=True)
    r_sub = lax.broadcasted_iota(jnp.int32, (tm, tm), 0).astype(F32)
    sort = jnp.where(r_sub == pos_row, 1.0, 0.0).astype(BF16)

    tot_i = tot.astype(jnp.int32)
    start = jnp.int32(0)
    for k in range(N_EXPERT_GROUPS):
        seg_ref[k] = start
        start = start + tot_i[0, GROUP_LANE + k]
        seg_ref[N_EXPERT_GROUPS + k] = start

    xs_scr[...] = _dot(sort, h_ref[...]).astype(BF16)
    g_hi, g_mid, g_lo = _split3(g)
    gs_scr[...] = _dot(sort, g_hi) + _dot(sort, g_mid) + _dot(sort, g_lo)
    acc_scr[...] = jnp.zeros(acc_scr.shape, F32)

    def block_group(p, carry):
        blk = lax.shift_right_logical(p, GROUP_COUNT_SHIFT)
        k = p & (N_EXPERT_GROUPS - 1)
        r0 = pl.multiple_of(blk * MOE_BLOCK, MOE_BLOCK)

        @pl.when((seg_ref[k] < r0 + MOE_BLOCK) & (seg_ref[N_EXPERT_GROUPS + k] > r0))
        def _():
            xb = xs_scr[pl.ds(r0, MOE_BLOCK), :]
            gsb = gs_scr[pl.ds(r0, MOE_BLOCK), :]
            lane_b = lax.broadcasted_iota(jnp.int32, gsb.shape, 1)
            y = None
            for j in range(EXPERTS_PER_GROUP):
                e = k * EXPERTS_PER_GROUP + j
                a = _dot(xb, wg_ref[e])
                u = _dot(xb, wu_ref[e])
                ge = jnp.sum(jnp.where(lane_b == e, gsb, 0.0), axis=-1, keepdims=True)
                part = _dot((a * _sigmoid(a) * u * ge).astype(BF16), wd_ref[e])
                y = part if y is None else y + part
            acc_scr[pl.ds(r0, MOE_BLOCK), :] += y

        return carry

    lax.fori_loop(0, (tm // MOE_BLOCK) * N_EXPERT_GROUPS, block_group, 0)

    a_hi, a_lo = _split2(acc_scr[...])
    x2 = x1_ref[...] + g2_ref[...] * (_dot(unsort, a_hi) + _dot(unsort, a_lo))
    if final:
        ms = jnp.mean(x2 * x2, axis=-1, keepdims=True)
        x2 = x2 * lax.rsqrt(ms + EPS) * fw_ref[...]
    o_ref[...] = x2


def _moe_constants(tm):
    r = np.arange(tm)
    tril = (r[None, :] <= r[:, None]).astype(np.float32)
    return jnp.asarray(tril, BF16), jnp.asarray(tril.T, BF16)


def _moe(h, gates, x1, mods, w_gate_b, w_up_b, w_down_b, final_w, layer, row_of_step, final):
    t = h.shape[0]
    tm = MOE_TILE
    tril, triu = _moe_constants(tm)
    row = lambda i: (i, 0)
    const = lambda i: (0, 0)
    resident = dict(pipeline_mode=pl.Buffered(1))
    return pl.pallas_call(
        functools.partial(_moe_kernel, final=final),
        grid=(t // tm,),
        in_specs=[
            pl.BlockSpec((tm, D_MODEL), row),
            pl.BlockSpec((tm, LANES), row),
            pl.BlockSpec((tm, D_MODEL), row),
            _mod_spec(layer, 5, row_of_step),
            pl.BlockSpec((None, N_EXPERTS, D_MODEL, D_FF), lambda i: (layer, 0, 0, 0), **resident),
            pl.BlockSpec((None, N_EXPERTS, D_MODEL, D_FF), lambda i: (layer, 0, 0, 0), **resident),
            pl.BlockSpec((None, N_EXPERTS, D_FF, D_MODEL), lambda i: (layer, 0, 0, 0), **resident),
            pl.BlockSpec((1, D_MODEL), const),
            pl.BlockSpec((tm, tm), const, **resident),
            pl.BlockSpec((tm, tm), const, **resident),
        ],
        out_specs=pl.BlockSpec((tm, D_MODEL), row),
        out_shape=jax.ShapeDtypeStruct((t, D_MODEL), F32),
        scratch_shapes=[
            pltpu.VMEM((tm, D_MODEL), BF16),
            pltpu.VMEM((tm, LANES), F32),
            pltpu.VMEM((tm, D_MODEL), F32),
            pltpu.SMEM((2 * N_EXPERT_GROUPS,), jnp.int32),
        ],
        compiler_params=_cparams(("arbitrary",)),
        name="moe_final" if final else "moe",
    )(h, gates, x1, mods, w_gate_b, w_up_b, w_down_b, final_w, tril, triu)


def _rope_tables(n_tokens):
    rows = n_tokens // GRID_W
    row = jnp.broadcast_to(jnp.arange(rows, dtype=F32)[:, None], (rows, GRID_W)).reshape(-1)
    col = jnp.broadcast_to(jnp.arange(GRID_W, dtype=F32)[None, :], (rows, GRID_W)).reshape(-1)
    n_freq = HEAD_DIM // 4
    inv_freq = jnp.power(ROPE_THETA, -jnp.arange(n_freq, dtype=F32) / n_freq)
    ang = jnp.concatenate([row[:, None] * inv_freq, col[:, None] * inv_freq], axis=-1)
    cos, sin = jnp.cos(ang), jnp.sin(ang)
    return jnp.tile(cos, (1, 4)), jnp.concatenate([-sin, sin, -sin, sin], axis=-1)


def kernel(x_prompt, x_sample, cache_k, cache_v, state_ssm, c, c_ctx, norm1_w, norm2_w, final_norm_w,
           w_ada, b_ada, w_in, conv_w, conv_b, attn_sink, dt_bias, a_log, d_skip, ssd_norm_w, w_out,
           w_router, router_bias, w_gate, w_up, w_down):
    bc, sc, d = x_prompt.shape
    bl, sl, _ = x_sample.shape
    depth = w_in.shape[0]
    assert d == D_MODEL and depth == DEPTH and bl + 1 <= MOD_ROWS
    assert sc % ROW_TILE == 0 or ROW_TILE % sc == 0
    assert sl % MOE_TILE == 0 and (bc * sc) % MOE_TILE == 0 and MOE_TILE % sc == 0

    cvec = jnp.zeros((MOD_ROWS, d), F32).at[0].set(c_ctx).at[1:1 + bl].set(c)
    mods = _adaln_all(cvec, w_ada, b_ada).reshape(depth, MOD_ROWS, 6, 1, d)

    scale = HEAD_DIM ** -0.5 * LOG2E
    w_in_p = jnp.concatenate(
        [w_in[:, :, :ATTN_WIDTH] * scale, w_in[:, :, ATTN_WIDTH:],
         jnp.zeros((depth, d, IN_PAD - w_in.shape[2]), F32)], axis=-1).astype(BF16)
    w_out_b = w_out.astype(BF16)
    w_gate_b, w_up_b, w_down_b = w_gate.astype(BF16), w_up.astype(BF16), w_down.astype(BF16)
    wr_hi = w_router.astype(BF16)
    wr_lo = (w_router - wr_hi.astype(F32)).astype(BF16)
    wr_p = jnp.concatenate([wr_hi, wr_lo, jnp.zeros((d, LANES - 2 * N_EXPERTS), BF16)], axis=-1)
    rb_col = router_bias.reshape(N_EXPERTS, 1)
    n1 = norm1_w.reshape(depth, 1, d)
    n2 = norm2_w.reshape(depth, 1, d)
    fw = final_norm_w.reshape(1, d)
    conv_w_p = jnp.concatenate([conv_w, jnp.zeros((depth, SUBLANES - D_CONV, XBC_WIDTH), F32)], axis=1)
    conv_b_p = conv_b.reshape(depth, 1, XBC_WIDTH)
    pad16 = lambda a: jnp.concatenate(
        [a.reshape(depth, 1, 2 * SSD_HEADS), jnp.zeros((depth, 1, LANES - 2 * SSD_HEADS), F32)], axis=-1)
    dtb_p = pad16(dt_bias)
    alog_p = pad16(a_log)
    dsk_p = jnp.repeat(d_skip, SSD_HEADDIM, axis=-1).reshape(depth, 1, SSD_WIDTH)
    snw_p = ssd_norm_w.reshape(depth, 1, SSD_WIDTH)
    rope_tabs = _rope_tables(sl)
    ck = cache_k.reshape(bl, depth, cache_k.shape[2], KV_WIDTH)
    cv = cache_v.reshape(bl, depth, cache_v.shape[2], KV_WIDTH)
    st_in = jnp.transpose(state_ssm, (0, 1, 2, 5, 3, 4)).reshape(bl, depth, 2, D_STATE, SSD_WIDTH)
    zero_state = jnp.zeros((bc, 2, D_STATE, SSD_WIDTH), F32)

    ctx_row = lambda i: 0
    lat_row_proj = lambda i: 1 + i // (sl // ROW_TILE)
    lat_row_moe = lambda i: 1 + i // (sl // MOE_TILE)

    xp = x_prompt.reshape(bc * sc, d)
    xs = x_sample.reshape(bl * sl, d)
    new_k, new_v, new_s = [], [], []
    for l in range(depth):
        last = l == depth - 1
        q, k, v, xbc, z, dt = _in_projection(xp, n1, mods, w_in_p, l, ctx_row, None, sc)
        attn = _attention_context(q, k, v, attn_sink, l, bc, sc)
        ssd, s_ctx = _ssd(xbc, z, dt, zero_state, conv_w_p, conv_b_p, dtb_p, alog_p, dsk_p, snw_p, l, bc, sc)
        x1, h2, gates = _out_projection(attn, ssd, xp, w_out_b, n2, mods, wr_p, rb_col, l, ctx_row)
        xp = _moe(h2, gates, x1, mods, w_gate_b, w_up_b, w_down_b, fw, l, ctx_row, last)
        new_k.append(k.reshape(bc, sc, N_KV_HEADS, HEAD_DIM))
        new_v.append(v.reshape(bc, sc, N_KV_HEADS, HEAD_DIM))
        new_s.append(jnp.transpose(s_ctx.reshape(bc, 2, D_STATE, SSD_HEADS, SSD_HEADDIM), (0, 1, 3, 4, 2)))
        q, k, v, xbc, z, dt = _in_projection(xs, n1, mods, w_in_p, l, lat_row_proj, rope_tabs, sl)
        attn = _attention_latent(q, k, v, ck, cv, attn_sink, l, bl, sl)
        ssd, _ = _ssd(xbc, z, dt, st_in[:, l], conv_w_p, conv_b_p, dtb_p, alog_p, dsk_p, snw_p, l, bl, sl)
        x1, h2, gates = _out_projection(attn, ssd, xs, w_out_b, n2, mods, wr_p, rb_col, l, lat_row_proj)
        xs = _moe(h2, gates, x1, mods, w_gate_b, w_up_b, w_down_b, fw, l, lat_row_moe, last)
    return (xp.reshape(bc, sc, d), xs.reshape(bl, sl, d),
            jnp.stack(new_k, axis=1), jnp.stack(new_v, axis=1), jnp.stack(new_s, axis=1))
```

```python
import functools

import numpy as np
import jax
import jax.numpy as jnp
from jax import lax
from jax.experimental import pallas as pl
from jax.experimental.pallas import tpu as pltpu

F32 = jnp.float32
BF16 = jnp.bfloat16

D_MODEL = 1024
DEPTH = 4
GRID_W = 64
EPS = 1e-6
NEG_INF = -1e30
N_HEADS = 8
N_KV_HEADS = 2
GQA_GROUP = N_HEADS // N_KV_HEADS
HEAD_DIM = 64
ATTN_WIDTH = N_HEADS * HEAD_DIM
KV_WIDTH = N_KV_HEADS * HEAD_DIM
BLOCK = 128
ROPE_THETA = 10000.0
SSD_HEADS = 8
SSD_HEADDIM = 64
SSD_WIDTH = SSD_HEADS * SSD_HEADDIM
SSD_GROUPS = 2
D_STATE = 64
BC_WIDTH = SSD_GROUPS * D_STATE
D_CONV = 5
CHUNK = 128
XBC_WIDTH = SSD_WIDTH + 2 * BC_WIDTH
N_EXPERTS = 16
N_EXPERT_GROUPS = 4
EXPERTS_PER_GROUP = N_EXPERTS // N_EXPERT_GROUPS
D_FF = 256
GROUP_SHIFT = EXPERTS_PER_GROUP.bit_length() - 1
assert 1 << GROUP_SHIFT == EXPERTS_PER_GROUP

LANES = 128
SUBLANES = 8
VMEM_LIMIT = 56 * 1024 * 1024

O_Q = 0
O_K = O_Q + ATTN_WIDTH
O_V = O_K + KV_WIDTH
O_XBC = O_V + KV_WIDTH
O_Z = O_XBC + XBC_WIDTH
O_DT = O_Z + SSD_WIDTH
IN_PAD = O_DT + LANES
MOD_ROWS = 16
ROW_TILE = 512
MOE_TILE = 512
MOE_BLOCK = 144
MOE_CHUNKS = MOE_TILE // MOE_BLOCK + N_EXPERT_GROUPS
MOE_ROWS = -(-MOE_CHUNKS * MOE_BLOCK // LANES) * LANES
GROUP_LANE = N_EXPERTS
GROUP_COUNT_SHIFT = N_EXPERT_GROUPS.bit_length() - 1
assert 1 << GROUP_COUNT_SHIFT == N_EXPERT_GROUPS
SSD_UNROLL = 4
LAT_QBLOCKS = 4
LOG2E = 1.4426950408889634


def _cparams(sem):
    return pltpu.CompilerParams(dimension_semantics=sem, vmem_limit_bytes=VMEM_LIMIT)


def _sigmoid(x):
    return 1.0 / (1.0 + jnp.exp(-x))


def _split2(x):
    hi = x.astype(BF16)
    lo = (x - hi.astype(F32)).astype(BF16)
    return hi, lo


def _split3(x):
    hi = x.astype(BF16)
    r = x - hi.astype(F32)
    mid = r.astype(BF16)
    lo = (r - mid.astype(F32)).astype(BF16)
    return hi, mid, lo


def _dot(a, b):
    return jnp.dot(a, b, preferred_element_type=F32)


def _dot_nt(a, b):
    return lax.dot_general(a, b, (((1,), (1,)), ((), ())), preferred_element_type=F32)


def _adaln_kernel(c_ref, w_ref, b_ref, o_ref):
    c = c_ref[...]
    s = c * _sigmoid(c)
    s_hi, s_lo = _split2(s)
    w = w_ref[...]
    w_hi, w_lo = _split2(w)
    acc = _dot(s_hi, w_hi) + _dot(s_lo, w_hi) + _dot(s_hi, w_lo)
    o_ref[...] = acc + b_ref[...]


def _adaln_all(cvec, w_ada, b_ada):
    depth, d, n = w_ada.shape
    tn = 1024
    return pl.pallas_call(
        _adaln_kernel,
        grid=(depth, n // tn),
        in_specs=[
            pl.BlockSpec((MOD_ROWS, d), lambda l, j: (0, 0)),
            pl.BlockSpec((None, d, tn), lambda l, j: (l, 0, j)),
            pl.BlockSpec((None, 1, tn), lambda l, j: (l, 0, j)),
        ],
        out_specs=pl.BlockSpec((None, MOD_ROWS, tn), lambda l, j: (l, 0, j)),
        out_shape=jax.ShapeDtypeStruct((depth, MOD_ROWS, n), F32),
        compiler_params=_cparams(("arbitrary", "arbitrary")),
        name="adaln",
    )(cvec, w_ada, b_ada.reshape(depth, 1, n))


def _mod_spec(layer, which, row_of_step):
    return pl.BlockSpec((None, None, None, 1, D_MODEL),
                        lambda i, *_: (layer, row_of_step(i), which, 0, 0))


def _rope(x, cos4, sin4):
    lane = lax.broadcasted_iota(jnp.int32, x.shape, 1)
    first_half = (lane & (HEAD_DIM - 1)) < (HEAD_DIM // 2)
    partner = jnp.where(first_half,
                        pltpu.roll(x, LANES - HEAD_DIM // 2, 1),
                        pltpu.roll(x, HEAD_DIM // 2, 1))
    return x * cos4 + partner * sin4


def _inproj_kernel(*refs, rope):
    if rope:
        (x_ref, nw_ref, sh_ref, sc_ref, w_ref, cos_ref, sin_ref,
         q_ref, k_ref, v_ref, xbc_ref, z_ref, dt_ref) = refs
    else:
        (x_ref, nw_ref, sh_ref, sc_ref, w_ref,
         q_ref, k_ref, v_ref, xbc_ref, z_ref, dt_ref) = refs
    x = x_ref[...]
    ms = jnp.mean(x * x, axis=-1, keepdims=True)
    y = x * lax.rsqrt(ms + EPS) * nw_ref[...]
    h = (y * (1.0 + sc_ref[...]) + sh_ref[...]).astype(BF16)

    def seg(a, b):
        return _dot(h, w_ref[:, a:b])

    if rope:
        cos4 = cos_ref[...]
        sin4 = sin_ref[...]
        for j in range(ATTN_WIDTH // LANES):
            q_ref[:, j * LANES:(j + 1) * LANES] = _rope(
                seg(O_Q + j * LANES, O_Q + (j + 1) * LANES), cos4, sin4).astype(q_ref.dtype)
        k_ref[...] = _rope(seg(O_K, O_V), cos4, sin4)
    else:
        q_ref[...] = seg(O_Q, O_K).astype(q_ref.dtype)
        k_ref[...] = seg(O_K, O_V)
    v_ref[...] = seg(O_V, O_XBC)
    xbc_ref[...] = seg(O_XBC, O_Z)
    z_ref[...] = seg(O_Z, O_DT)
    dt_ref[...] = seg(O_DT, IN_PAD)


def _in_projection(x, norm_w, mods, w_in_p, layer, row_of_step, rope_tabs, seq_len):
    t = x.shape[0]
    tm = ROW_TILE
    rope = rope_tabs is not None
    row = lambda i: (i, 0)
    in_specs = [
        pl.BlockSpec((tm, D_MODEL), row),
        pl.BlockSpec((None, 1, D_MODEL), lambda i: (layer, 0, 0)),
        _mod_spec(layer, 0, row_of_step),
        _mod_spec(layer, 1, row_of_step),
        pl.BlockSpec((None, D_MODEL, IN_PAD), lambda i: (layer, 0, 0)),
    ]
    args = [x, norm_w, mods, mods, w_in_p]
    if rope:
        steps_per_seq = seq_len // tm
        tab = pl.BlockSpec((tm, LANES), lambda i: (i % steps_per_seq, 0))
        in_specs += [tab, tab]
        args += list(rope_tabs)
    widths = (ATTN_WIDTH, KV_WIDTH, KV_WIDTH, XBC_WIDTH, SSD_WIDTH, LANES)
    dtypes = (BF16, F32, F32, F32, F32, F32)
    return pl.pallas_call(
        functools.partial(_inproj_kernel, rope=rope),
        grid=(t // tm,),
        in_specs=in_specs,
        out_specs=[pl.BlockSpec((tm, w), row) for w in widths],
        out_shape=[jax.ShapeDtypeStruct((t, w), dt) for w, dt in zip(widths, dtypes)],
        compiler_params=_cparams(("arbitrary",)),
        name="in_projection_rope" if rope else "in_projection",
    )(*args)


def _kv_operands(k, v):
    low = lax.broadcasted_iota(jnp.int32, k.shape, 1) < HEAD_DIM
    k_sw = pltpu.roll(k, HEAD_DIM, 1)
    v_sw = pltpu.roll(v, HEAD_DIM, 1)
    kdup = (jnp.where(low, k, k_sw).astype(BF16), jnp.where(low, k_sw, k).astype(BF16))
    vdup = (jnp.where(low, v, v_sw).astype(BF16), jnp.where(low, v_sw, v).astype(BF16))
    return kdup, vdup


def _attend_group(sink_ref, q_ref, o_ref, layer, g, r0, tq, segments):
    low = lax.broadcasted_iota(jnp.int32, (tq, LANES), 1) < HEAD_DIM
    base = g * GQA_GROUP * HEAD_DIM
    rows = []
    for j in range(GQA_GROUP // 2):
        slab = q_ref[r0:r0 + tq, base + j * LANES:base + (j + 1) * LANES]
        zero = jnp.zeros_like(slab)
        rows += [jnp.where(low, slab, zero), jnp.where(low, zero, slab)]
    lhs = jnp.concatenate(rows, axis=0)
    sink = jnp.concatenate(
        [jnp.full((tq, LANES), sink_ref[layer, g * GQA_GROUP + i] * LOG2E, F32) for i in range(GQA_GROUP)],
        axis=0)
    tiles = []
    for kdup, _, patches in segments:
        s = _dot_nt(lhs, kdup)
        ts = [s[:, c * LANES:(c + 1) * LANES] for c in range(s.shape[1] // LANES)]
        for c, mask in patches or ():
            ts[c] = (ts[c].reshape(GQA_GROUP, tq, LANES) + mask[None]).reshape(ts[c].shape)
        tiles.append(ts)
    flat = [t for ts in tiles for t in ts]
    m = jnp.max(functools.reduce(jnp.maximum, flat + [sink]), axis=-1, keepdims=True)
    probs = [[jnp.exp2(t - m) for t in ts] for ts in tiles]
    total = functools.reduce(jnp.add, [p for ps in probs for p in ps])
    lane0 = lax.broadcasted_iota(jnp.int32, sink.shape, 1) == 0
    total = total + jnp.where(lane0, jnp.exp2(sink - m), 0.0)
    denom = jnp.sum(total, axis=-1, keepdims=True)
    o = None
    for ps, (_, vdup, _) in zip(probs, segments):
        part = _dot(jnp.concatenate(ps, axis=-1).astype(BF16), vdup)
        o = part if o is None else o + part
    o = o * (1.0 / denom)
    for j in range(GQA_GROUP // 2):
        pair = jnp.where(low, o[2 * j * tq:(2 * j + 1) * tq], o[(2 * j + 1) * tq:(2 * j + 2) * tq])
        o_ref[r0:r0 + tq, base + j * LANES:base + (j + 1) * LANES] = pair.astype(o_ref.dtype)


def _attn_ctx_kernel(sink_ref, q_ref, k_ref, v_ref, o_ref, *, layer):
    kdup, vdup = _kv_operands(k_ref[...], v_ref[...])
    for x in range(q_ref.shape[0] // BLOCK):
        for g in range(N_KV_HEADS):
            _attend_group(sink_ref, q_ref, o_ref, layer, g, x * BLOCK, BLOCK, [(kdup[g], vdup[g], None)])


def _attention_context(q, k, v, attn_sink, layer, batch, seq):
    row = lambda b: (b, 0)
    return pl.pallas_call(
        functools.partial(_attn_ctx_kernel, layer=layer),
        grid=(batch,),
        in_specs=[
            pl.BlockSpec(memory_space=pltpu.SMEM),
            pl.BlockSpec((seq, ATTN_WIDTH), row),
            pl.BlockSpec((seq, KV_WIDTH), row),
            pl.BlockSpec((seq, KV_WIDTH), row),
        ],
        out_specs=pl.BlockSpec((seq, ATTN_WIDTH), row),
        out_shape=jax.ShapeDtypeStruct((batch * seq, ATTN_WIDTH), BF16),
        compiler_params=_cparams(("arbitrary",)),
        name="attention_context",
    )(attn_sink, q, k, v)


def _attn_lat_kernel(sink_ref, q_ref, k_ref, v_ref, kc_ref, vc_ref, o_ref, *, layer, n_blocks):
    first_block = pl.program_id(1) * LAT_QBLOCKS
    band_blocks = LAT_QBLOCKS + 2

    def band(seq_ref):
        parts = []
        for i in range(band_blocks):
            blk = jnp.clip(first_block - 1 + i, 0, n_blocks - 1)
            parts.append(seq_ref[pl.ds(pl.multiple_of(blk * BLOCK, BLOCK), BLOCK), :])
        return jnp.concatenate(parts, axis=0)

    past_tiles = kc_ref.shape[0] // BLOCK
    kdup, vdup = _kv_operands(jnp.concatenate([kc_ref[...], band(k_ref)], axis=0),
                              jnp.concatenate([vc_ref[...], band(v_ref)], axis=0))
    qi = lax.broadcasted_iota(jnp.int32, (BLOCK, BLOCK), 0)
    kj = lax.broadcasted_iota(jnp.int32, (BLOCK, BLOCK), 1)
    for x in range(LAT_QBLOCKS):
        blk = first_block + x
        mask_prev = jnp.where(kj >= qi + jnp.where(blk > 0, 0, BLOCK), 0.0, NEG_INF)
        mask_next = jnp.where(kj <= qi - jnp.where(blk < n_blocks - 1, 0, BLOCK), 0.0, NEG_INF)
        patches = [(past_tiles, mask_prev), (past_tiles + 2, mask_next)]
        for g in range(N_KV_HEADS):
            if LAT_QBLOCKS == 1:
                keys, vals = kdup[g], vdup[g]
            else:
                pick = lambda a: jnp.concatenate(
                    [a[:past_tiles * BLOCK], a[(past_tiles + x) * BLOCK:(past_tiles + x + 3) * BLOCK]], axis=0)
                keys, vals = pick(kdup[g]), pick(vdup[g])
            _attend_group(sink_ref, q_ref, o_ref, layer, g, x * BLOCK, BLOCK, [(keys, vals, patches)])


def _attention_latent(q, k, v, cache_k, cache_v, attn_sink, layer, batch, seq):
    n_blocks = seq // BLOCK
    steps = n_blocks // LAT_QBLOCKS
    tq = LAT_QBLOCKS * BLOCK
    past = cache_k.shape[2]
    ctx = pl.BlockSpec((None, None, past, KV_WIDTH), lambda b, n: (b, layer, 0, 0))
    whole = pl.BlockSpec((seq, KV_WIDTH), lambda b, n: (b, 0))
    qo = pl.BlockSpec((tq, ATTN_WIDTH), lambda b, n: (b * steps + n, 0))
    return pl.pallas_call(
        functools.partial(_attn_lat_kernel, layer=layer, n_blocks=n_blocks),
        grid=(batch, steps),
        in_specs=[pl.BlockSpec(memory_space=pltpu.SMEM), qo, whole, whole, ctx, ctx],
        out_specs=qo,
        out_shape=jax.ShapeDtypeStruct((batch * seq, ATTN_WIDTH), BF16),
        compiler_params=_cparams(("arbitrary", "arbitrary")),
        name="attention_latent",
    )(attn_sink, q, k, v, cache_k, cache_v)


def _ssd_kernel(xbc_ref, z_ref, dt_ref, init_ref, cw_ref, cb_ref, dtb_ref, alog_ref, dsk_ref, nw_ref,
                tri_ref, exp_ref, y_ref, st_ref, pad_scr, act_scr, sb_scr, sf_scr, sr_scr, *, seq):
    n_chunks = seq // CHUNK
    half = SSD_WIDTH // SSD_GROUPS
    pad_lo = SUBLANES
    pad_scr[0:pad_lo, :] = jnp.zeros((pad_lo, XBC_WIDTH), F32)
    pad_scr[pad_lo:pad_lo + seq, :] = xbc_ref[...]
    pad_scr[pad_lo + seq:2 * pad_lo + seq, :] = jnp.zeros((pad_lo, XBC_WIDTH), F32)

    lane = lax.broadcasted_iota(jnp.int32, (CHUNK, LANES), 1)
    is_fwd = lane < SSD_HEADS
    is_head = lane < 2 * SSD_HEADS
    ti = lax.broadcasted_iota(jnp.int32, (CHUNK, CHUNK), 0)
    tj = lax.broadcasted_iota(jnp.int32, (CHUNK, CHUNK), 1)
    lower = tj <= ti
    upper = tj >= ti
    a_row = jnp.where(is_head[0:1, :], -jnp.exp(alog_ref[...]), 0.0)

    def conv_silu(r0):
        win = pad_scr[pl.ds(r0, CHUNK + 2 * pad_lo), :]
        first = pad_lo - (D_CONV - 1) // 2
        acc = cb_ref[...] + cw_ref[0:1, :] * win[first:first + CHUNK, :]
        for kk in range(1, D_CONV):
            acc = acc + cw_ref[kk:kk + 1, :] * win[first + kk:first + kk + CHUNK, :]
        return acc * _sigmoid(acc)

    def prep(r0):
        x = dt_ref[pl.ds(r0, CHUNK), :] + dtb_ref[...]
        dtv = jnp.maximum(x, 0.0) + jnp.log(1.0 + jnp.exp(-jnp.abs(x)))
        da = dtv * a_row
        hi, mid, lo = _split3(da)
        tri = tri_ref[...]
        cs = _dot(tri, hi) + _dot(tri, mid) + _dot(tri, lo)
        ecs = cs - da
        tot = cs[CHUNK - 1:CHUNK, :]
        ldt = jnp.log(dtv)
        col_q = jnp.where(is_fwd, cs, ecs)
        row_q = jnp.where(is_fwd, cs - ldt, ecs + ldt).T
        scale_y = jnp.where(is_fwd, jnp.exp(cs), jnp.exp(tot - ecs))
        scale_s = dtv * jnp.where(is_fwd, jnp.exp(tot - cs), jnp.exp(ecs))
        ex = exp_ref[...]
        scale_y = _dot(scale_y.astype(BF16), ex)
        scale_s = _dot(scale_s.astype(BF16), ex)
        d_hi, d_mid, d_lo = _split3(jnp.broadcast_to(jnp.exp(tot), (SUBLANES, LANES)))
        decay = (_dot(d_hi, ex) + _dot(d_mid, ex) + _dot(d_lo, ex))[0:1, :]
        return col_q, row_q, scale_y, scale_s, decay

    def chunk_state(act, weights, decay, state):
        xw = (act[:, 0:SSD_WIDTH] * weights).astype(BF16)
        bt = act[:, SSD_WIDTH:SSD_WIDTH + BC_WIDTH].T.astype(BF16)
        upd = jnp.concatenate(
            [_dot(bt[g * D_STATE:(g + 1) * D_STATE, :], xw[:, g * half:(g + 1) * half])
             for g in range(SSD_GROUPS)], axis=-1)
        return decay * state + upd

    sr_scr[...] = init_ref[1]

    def bwd_body(i, carry):
        c = n_chunks - 1 - i
        r0 = pl.multiple_of(c * CHUNK, CHUNK)
        act = conv_silu(r0)
        act_scr[pl.ds(r0, CHUNK), :] = act
        _, _, _, scale_s, decay = prep(r0)
        state = sr_scr[...]
        sb_scr[c] = state
        sr_scr[...] = chunk_state(act, scale_s[:, SSD_WIDTH:], decay[:, SSD_WIDTH:], state)
        return carry

    lax.fori_loop(0, n_chunks, bwd_body, 0, unroll=min(SSD_UNROLL, n_chunks))
    st_ref[1] = sr_scr[...]

    sf_scr[...] = init_ref[0]

    def fwd_body(c, carry):
        r0 = pl.multiple_of(c * CHUNK, CHUNK)
        act = act_scr[pl.ds(r0, CHUNK), :]
        col_q, row_q, scale_y, scale_s, decay = prep(r0)
        xs = act[:, 0:SSD_WIDTH]
        xs_b = xs.astype(BF16)
        bmat = act[:, SSD_WIDTH:SSD_WIDTH + BC_WIDTH].astype(BF16)
        cmat = act[:, SSD_WIDTH + BC_WIDTH:XBC_WIDTH].astype(BF16)
        s_f = sf_scr[...]
        s_b = sb_scr[c]
        y_parts = []
        for g in range(SSD_GROUPS):
            gs = slice(g * D_STATE, (g + 1) * D_STATE)
            cb = _dot_nt(cmat[:, gs], bmat[:, gs])
            for pair in range(SSD_HEADS // SSD_GROUPS // 2):
                ws = []
                for h in (g * 4 + 2 * pair, g * 4 + 2 * pair + 1):
                    hb = SSD_HEADS + h
                    e_f = jnp.exp(jnp.where(lower, col_q[:, h:h + 1] - row_q[h:h + 1, :], NEG_INF))
                    e_b = jnp.exp(jnp.where(upper, row_q[hb:hb + 1, :] - col_q[:, hb:hb + 1], NEG_INF))
                    ws.append((cb * (e_f + e_b)).astype(BF16))
                h0 = g * 4 + 2 * pair
                slab = xs_b[:, h0 * SSD_HEADDIM:(h0 + 2) * SSD_HEADDIM]
                first = lane < SSD_HEADDIM
                rhs = jnp.concatenate([jnp.where(first, slab, jnp.zeros_like(slab)),
                                       jnp.where(first, jnp.zeros_like(slab), slab)], axis=0)
                y_parts.append(_dot(jnp.concatenate(ws, axis=-1), rhs))
        y = jnp.concatenate(y_parts, axis=-1)
        off_f = jnp.concatenate(
            [_dot(cmat[:, g * D_STATE:(g + 1) * D_STATE], s_f[:, g * half:(g + 1) * half].astype(BF16))
             for g in range(SSD_GROUPS)], axis=-1)
        off_b = jnp.concatenate(
            [_dot(cmat[:, g * D_STATE:(g + 1) * D_STATE], s_b[:, g * half:(g + 1) * half].astype(BF16))
             for g in range(SSD_GROUPS)], axis=-1)
        y = y + off_f * scale_y[:, 0:SSD_WIDTH] + off_b * scale_y[:, SSD_WIDTH:] + dsk_ref[...] * xs
        zc = z_ref[pl.ds(r0, CHUNK), :]
        gated = y * (zc * _sigmoid(zc))
        outs = []
        for g in range(SSD_GROUPS):
            gg = gated[:, g * half:(g + 1) * half]
            outs.append(gg * lax.rsqrt(jnp.mean(gg * gg, axis=-1, keepdims=True) + EPS))
        y_ref[pl.ds(r0, CHUNK), :] = (jnp.concatenate(outs, axis=-1) * nw_ref[...]).astype(y_ref.dtype)
        sf_scr[...] = chunk_state(act, scale_s[:, 0:SSD_WIDTH], decay[:, 0:SSD_WIDTH], s_f)
        return carry

    lax.fori_loop(0, n_chunks, fwd_body, 0, unroll=min(SSD_UNROLL, n_chunks))
    st_ref[0] = sf_scr[...]


def _ssd_constants():
    r = np.arange(CHUNK)
    tri = (r[None, :] <= r[:, None]).astype(np.float32)
    cols = np.arange(2 * SSD_WIDTH)
    expand = (cols[None, :] // SSD_HEADDIM == np.arange(LANES)[:, None]).astype(np.float32)
    return jnp.asarray(tri, BF16), jnp.asarray(expand, BF16)


def _ssd(xbc, z, dt, init, conv_w, conv_b, dt_bias, a_log, d_skip, ssd_norm_w, layer, batch, seq):
    tri, expand = _ssd_constants()
    n_chunks = seq // CHUNK
    row = lambda b: (b, 0)
    lay = lambda b: (layer, 0, 0)
    const = lambda b: (0, 0)
    state_spec = pl.BlockSpec((None, 2, D_STATE, SSD_WIDTH), lambda b: (b, 0, 0, 0))
    return pl.pallas_call(
        functools.partial(_ssd_kernel, seq=seq),
        grid=(batch,),
        in_specs=[
            pl.BlockSpec((seq, XBC_WIDTH), row),
            pl.BlockSpec((seq, SSD_WIDTH), row),
            pl.BlockSpec((seq, LANES), row),
            state_spec,
            pl.BlockSpec((None, SUBLANES, XBC_WIDTH), lay),
            pl.BlockSpec((None, 1, XBC_WIDTH), lay),
            pl.BlockSpec((None, 1, LANES), lay),
            pl.BlockSpec((None, 1, LANES), lay),
            pl.BlockSpec((None, 1, SSD_WIDTH), lay),
            pl.BlockSpec((None, 1, SSD_WIDTH), lay),
            pl.BlockSpec((CHUNK, CHUNK), const),
            pl.BlockSpec((LANES, 2 * SSD_WIDTH), const),
        ],
        out_specs=[pl.BlockSpec((seq, SSD_WIDTH), row), state_spec],
        out_shape=[jax.ShapeDtypeStruct((batch * seq, SSD_WIDTH), BF16),
                   jax.ShapeDtypeStruct((batch, 2, D_STATE, SSD_WIDTH), F32)],
        scratch_shapes=[
            pltpu.VMEM((seq + 2 * SUBLANES, XBC_WIDTH), F32),
            pltpu.VMEM((seq, XBC_WIDTH), F32),
            pltpu.VMEM((n_chunks, D_STATE, SSD_WIDTH), F32),
            pltpu.VMEM((D_STATE, SSD_WIDTH), F32),
            pltpu.VMEM((D_STATE, SSD_WIDTH), F32),
        ],
        compiler_params=_cparams(("arbitrary",)),
        name="ssd",
    )(xbc, z, dt, init, conv_w, conv_b, dt_bias, a_log, d_skip, ssd_norm_w, tri, expand)


def _first_argmax(rows):
    best_v = rows[0]
    best_i = jnp.zeros(rows[0].shape, jnp.int32)
    for i in range(1, len(rows)):
        better = rows[i] > best_v
        best_v = jnp.where(better, rows[i], best_v)
        best_i = jnp.where(better, i, best_i)
    return best_i, best_v


def _outproj_kernel(attn_ref, ssd_ref, x_ref, w_ref, g1_ref, nw_ref, sh_ref, sc_ref, wr_ref, rb_ref,
                    x1_ref, h_ref, gates_ref):
    half = ATTN_WIDTH
    mix = _dot(attn_ref[...], w_ref[0:half, :]) + _dot(ssd_ref[...], w_ref[half:, :])
    x1 = x_ref[...] + g1_ref[...] * mix
    x1_ref[...] = x1
    ms = jnp.mean(x1 * x1, axis=-1, keepdims=True)
    y = x1 * lax.rsqrt(ms + EPS) * nw_ref[...]
    h = y * (1.0 + sc_ref[...]) + sh_ref[...]
    h_hi, h_lo = _split2(h)
    h_ref[...] = h_hi
    wr = wr_ref[...]
    l1 = _dot(h_hi, wr).T
    l2 = _dot(h_lo, wr).T
    ne = N_EXPERTS
    logits = l1[0:ne, :] + l1[ne:2 * ne, :] + l2[0:ne, :]
    scores = _sigmoid(logits)
    sel = scores + rb_ref[...]
    srow = [sel[e:e + 1, :] for e in range(ne)]
    group_score = []
    for g in range(N_EXPERT_GROUPS):
        r = srow[g * EXPERTS_PER_GROUP:(g + 1) * EXPERTS_PER_GROUP]
        pairs = [r[i] + r[j] for i in range(len(r)) for j in range(i + 1, len(r))]
        top2 = pairs[0]
        for p in pairs[1:]:
            top2 = jnp.maximum(top2, p)
        group_score.append(top2)
    best_group, _ = _first_argmax(group_score)
    eid = lax.broadcasted_iota(jnp.int32, sel.shape, 0)
    masked = jnp.where(lax.shift_right_logical(eid, GROUP_SHIFT) == best_group, sel, NEG_INF)
    i1, _ = _first_argmax([masked[e:e + 1, :] for e in range(ne)])
    masked2 = jnp.where(eid == i1, -jnp.inf, masked)
    i2, _ = _first_argmax([masked2[e:e + 1, :] for e in range(ne)])
    pick1 = eid == i1
    pick2 = eid == i2
    w1 = jnp.sum(jnp.where(pick1, scores, 0.0), axis=0, keepdims=True)
    w2 = jnp.sum(jnp.where(pick2, scores, 0.0), axis=0, keepdims=True)
    wsum = w1 + w2
    gates = jnp.where(pick1, w1 / wsum, 0.0) + jnp.where(pick2, w2 / wsum, 0.0)
    gid = lax.broadcasted_iota(jnp.int32, (N_EXPERT_GROUPS, gates.shape[1]), 0)
    onehot = jnp.where(gid == best_group, 1.0, 0.0)
    pad = jnp.zeros((LANES - ne - N_EXPERT_GROUPS, gates.shape[1]), F32)
    gates_ref[...] = jnp.concatenate([gates, onehot, pad], axis=0).T


def _out_projection(attn, ssd, x, w_out_b, norm_w, mods, wr_p, rb_col, layer, row_of_step):
    t = x.shape[0]
    tm = ROW_TILE
    row = lambda i: (i, 0)
    return pl.pallas_call(
        _outproj_kernel,
        grid=(t // tm,),
        in_specs=[
            pl.BlockSpec((tm, ATTN_WIDTH), row),
            pl.BlockSpec((tm, SSD_WIDTH), row),
            pl.BlockSpec((tm, D_MODEL), row),
            pl.BlockSpec((None, D_MODEL, D_MODEL), lambda i: (layer, 0, 0)),
            _mod_spec(layer, 2, row_of_step),
            pl.BlockSpec((None, 1, D_MODEL), lambda i: (layer, 0, 0)),
            _mod_spec(layer, 3, row_of_step),
            _mod_spec(layer, 4, row_of_step),
            pl.BlockSpec((D_MODEL, LANES), lambda i: (0, 0)),
            pl.BlockSpec((N_EXPERTS, 1), lambda i: (0, 0)),
        ],
        out_specs=[pl.BlockSpec((tm, D_MODEL), row), pl.BlockSpec((tm, D_MODEL), row),
                   pl.BlockSpec((tm, LANES), row)],
        out_shape=[jax.ShapeDtypeStruct((t, D_MODEL), F32), jax.ShapeDtypeStruct((t, D_MODEL), BF16),
                   jax.ShapeDtypeStruct((t, LANES), F32)],
        compiler_params=_cparams(("arbitrary",)),
        name="out_projection",
    )(attn, ssd, x, w_out_b, mods, norm_w, mods, mods, wr_p, rb_col)


def _moe_kernel(h_ref, gates_ref, x1_ref, g2_ref, wg_ref, wu_ref, wd_ref, fw_ref, tril_ref, triu_ref,
                o_ref, p_scr, acc_scr, seg_ref, *, final):
    tm = h_ref.shape[0]
    rows = p_scr.shape[0]
    g = gates_ref[...]
    lane = lax.broadcasted_iota(jnp.int32, g.shape, 1)
    onehot = jnp.where((lane >= GROUP_LANE) & (lane < GROUP_LANE + N_EXPERT_GROUPS), g, 0.0)
    cnt = _dot(tril_ref[...], onehot.astype(BF16))
    tot_i = cnt[tm - 1:tm, :].astype(jnp.int32)
    chunk0 = jnp.int32(0)
    offs = []
    for k in range(N_EXPERT_GROUPS):
        offs.append((chunk0 * MOE_BLOCK).astype(F32))
        chunk0 = chunk0 + (tot_i[0, GROUP_LANE + k] + (MOE_BLOCK - 1)) // MOE_BLOCK
        seg_ref[k] = chunk0
    n_chunks = chunk0

    lane1 = lax.broadcasted_iota(jnp.int32, (1, LANES), 1)
    off_row = functools.reduce(jnp.add, [jnp.where(lane1 == GROUP_LANE + k, offs[k], 0.0)
                                         for k in range(N_EXPERT_GROUPS)])
    pos_col = jnp.sum(onehot * (cnt + off_row - 1.0), axis=-1, keepdims=True)
    r_lane = lax.broadcasted_iota(jnp.int32, (tm, rows), 1).astype(F32)
    unsort = jnp.where(r_lane == pos_col, 1.0, 0.0).astype(BF16)
    onehot_t = g.T[GROUP_LANE:GROUP_LANE + SUBLANES, :]
    sub = lax.broadcasted_iota(jnp.int32, onehot_t.shape, 0)
    onehot_t = jnp.where(sub < N_EXPERT_GROUPS, onehot_t, 0.0)
    cnt_t = _dot(onehot_t.astype(BF16), triu_ref[...])
    off_t = functools.reduce(jnp.add, [jnp.where(sub == k, offs[k], 0.0) for k in range(N_EXPERT_GROUPS)])
    pos_row = jnp.sum(onehot_t * (cnt_t + off_t - 1.0), axis=0, keepdims=True)
    r_sub = lax.broadcasted_iota(jnp.int32, (rows, tm), 0).astype(F32)
    p_scr[...] = jnp.where(r_sub == pos_row, 1.0, 0.0).astype(BF16)
    acc_scr[...] = jnp.zeros(acc_scr.shape, acc_scr.dtype)
    g_hi, g_lo = _split2(g)

    def chunk(c, carry):
        @pl.when(c < n_chunks)
        def _():
            k = functools.reduce(jnp.add, [jnp.where(c >= seg_ref[i], 1, 0)
                                           for i in range(N_EXPERT_GROUPS - 1)])
            r0 = pl.multiple_of(c * MOE_BLOCK, 16)
            pc = p_scr[pl.ds(r0, MOE_BLOCK), :]
            xb = _dot(pc, h_ref[...]).astype(BF16)
            gsb = _dot(pc, g_hi) + _dot(pc, g_lo)
            lane_b = lax.broadcasted_iota(jnp.int32, gsb.shape, 1)
            y = None
            for j in range(EXPERTS_PER_GROUP):
                e = k * EXPERTS_PER_GROUP + j
                a = _dot(xb, wg_ref[e])
                u = _dot(xb, wu_ref[e])
                ge = jnp.sum(jnp.where(lane_b == e, gsb, 0.0), axis=-1, keepdims=True)
                part = _dot((a * _sigmoid(a) * u * ge).astype(BF16), wd_ref[e])
                y = part if y is None else y + part
            acc_scr[pl.ds(r0, MOE_BLOCK), :] = y.astype(acc_scr.dtype)

        return carry

    lax.fori_loop(0, MOE_CHUNKS, chunk, 0)

    x2 = x1_ref[...] + g2_ref[...] * _dot(unsort, acc_scr[...])
    if final:
        ms = jnp.mean(x2 * x2, axis=-1, keepdims=True)
        x2 = x2 * lax.rsqrt(ms + EPS) * fw_ref[...]
    o_ref[...] = x2


def _moe_constants(tm):
    r = np.arange(tm)
    tril = (r[None, :] <= r[:, None]).astype(np.float32)
    return jnp.asarray(tril, BF16), jnp.asarray(tril.T, BF16)


def _moe(h, gates, x1, mods, w_gate_b, w_up_b, w_down_b, final_w, layer, row_of_step, final):
    t = h.shape[0]
    tm = MOE_TILE
    tril, triu = _moe_constants(tm)
    row = lambda i: (i, 0)
    const = lambda i: (0, 0)
    resident = dict(pipeline_mode=pl.Buffered(1))
    return pl.pallas_call(
        functools.partial(_moe_kernel, final=final),
        grid=(t // tm,),
        in_specs=[
            pl.BlockSpec((tm, D_MODEL), row),
            pl.BlockSpec((tm, LANES), row),
            pl.BlockSpec((tm, D_MODEL), row),
            _mod_spec(layer, 5, row_of_step),
            pl.BlockSpec((None, N_EXPERTS, D_MODEL, D_FF), lambda i: (layer, 0, 0, 0), **resident),
            pl.BlockSpec((None, N_EXPERTS, D_MODEL, D_FF), lambda i: (layer, 0, 0, 0), **resident),
            pl.BlockSpec((None, N_EXPERTS, D_FF, D_MODEL), lambda i: (layer, 0, 0, 0), **resident),
            pl.BlockSpec((1, D_MODEL), const),
            pl.BlockSpec((tm, tm), const, **resident),
            pl.BlockSpec((tm, tm), const, **resident),
        ],
        out_specs=pl.BlockSpec((tm, D_MODEL), row),
        out_shape=jax.ShapeDtypeStruct((t, D_MODEL), F32),
        scratch_shapes=[
            pltpu.VMEM((MOE_ROWS, tm), BF16),
            pltpu.VMEM((MOE_ROWS, D_MODEL), BF16),
            pltpu.SMEM((N_EXPERT_GROUPS,), jnp.int32),
        ],
        compiler_params=_cparams(("arbitrary",)),
        name="moe_final" if final else "moe",
    )(h, gates, x1, mods, w_gate_b, w_up_b, w_down_b, final_w, tril, triu)


def _rope_tables(n_tokens):
    rows = n_tokens // GRID_W
    row = jnp.broadcast_to(jnp.arange(rows, dtype=F32)[:, None], (rows, GRID_W)).reshape(-1)
    col = jnp.broadcast_to(jnp.arange(GRID_W, dtype=F32)[None, :], (rows, GRID_W)).reshape(-1)
    n_freq = HEAD_DIM // 4
    inv_freq = jnp.power(ROPE_THETA, -jnp.arange(n_freq, dtype=F32) / n_freq)
    ang = jnp.concatenate([row[:, None] * inv_freq, col[:, None] * inv_freq], axis=-1)
    cos, sin = jnp.cos(ang), jnp.sin(ang)
    return jnp.tile(cos, (1, 4)), jnp.concatenate([-sin, sin, -sin, sin], axis=-1)


def kernel(x_prompt, x_sample, cache_k, cache_v, state_ssm, c, c_ctx, norm1_w, norm2_w, final_norm_w,
           w_ada, b_ada, w_in, conv_w, conv_b, attn_sink, dt_bias, a_log, d_skip, ssd_norm_w, w_out,
           w_router, router_bias, w_gate, w_up, w_down):
    bc, sc, d = x_prompt.shape
    bl, sl, _ = x_sample.shape
    depth = w_in.shape[0]
    assert d == D_MODEL and depth == DEPTH and bl + 1 <= MOD_ROWS
    assert sc % ROW_TILE == 0 or ROW_TILE % sc == 0
    assert sl % MOE_TILE == 0 and (bc * sc) % MOE_TILE == 0 and MOE_TILE % sc == 0

    cvec = jnp.zeros((MOD_ROWS, d), F32).at[0].set(c_ctx).at[1:1 + bl].set(c)
    mods = _adaln_all(cvec, w_ada, b_ada).reshape(depth, MOD_ROWS, 6, 1, d)

    scale = HEAD_DIM ** -0.5 * LOG2E
    w_in_p = jnp.concatenate(
        [w_in[:, :, :ATTN_WIDTH] * scale, w_in[:, :, ATTN_WIDTH:],
         jnp.zeros((depth, d, IN_PAD - w_in.shape[2]), F32)], axis=-1).astype(BF16)
    w_out_b = w_out.astype(BF16)
    w_gate_b, w_up_b, w_down_b = w_gate.astype(BF16), w_up.astype(BF16), w_down.astype(BF16)
    wr_hi = w_router.astype(BF16)
    wr_lo = (w_router - wr_hi.astype(F32)).astype(BF16)
    wr_p = jnp.concatenate([wr_hi, wr_lo, jnp.zeros((d, LANES - 2 * N_EXPERTS), BF16)], axis=-1)
    rb_col = router_bias.reshape(N_EXPERTS, 1)
    n1 = norm1_w.reshape(depth, 1, d)
    n2 = norm2_w.reshape(depth, 1, d)
    fw = final_norm_w.reshape(1, d)
    conv_w_p = jnp.concatenate([conv_w, jnp.zeros((depth, SUBLANES - D_CONV, XBC_WIDTH), F32)], axis=1)
    conv_b_p = conv_b.reshape(depth, 1, XBC_WIDTH)
    pad16 = lambda a: jnp.concatenate(
        [a.reshape(depth, 1, 2 * SSD_HEADS), jnp.zeros((depth, 1, LANES - 2 * SSD_HEADS), F32)], axis=-1)
    dtb_p = pad16(dt_bias)
    alog_p = pad16(a_log)
    dsk_p = jnp.repeat(d_skip, SSD_HEADDIM, axis=-1).reshape(depth, 1, SSD_WIDTH)
    snw_p = ssd_norm_w.reshape(depth, 1, SSD_WIDTH)
    rope_tabs = _rope_tables(sl)
    ck = cache_k.reshape(bl, depth, cache_k.shape[2], KV_WIDTH)
    cv = cache_v.reshape(bl, depth, cache_v.shape[2], KV_WIDTH)
    st_in = jnp.transpose(state_ssm, (0, 1, 2, 5, 3, 4)).reshape(bl, depth, 2, D_STATE, SSD_WIDTH)
    zero_state = jnp.zeros((bc, 2, D_STATE, SSD_WIDTH), F32)

    ctx_row = lambda i: 0
    lat_row_proj = lambda i: 1 + i // (sl // ROW_TILE)
    lat_row_moe = lambda i: 1 + i // (sl // MOE_TILE)

    xp = x_prompt.reshape(bc * sc, d)
    xs = x_sample.reshape(bl * sl, d)
    new_k, new_v, new_s = [], [], []
    for l in range(depth):
        last = l == depth - 1
        q, k, v, xbc, z, dt = _in_projection(xp, n1, mods, w_in_p, l, ctx_row, None, sc)
        attn = _attention_context(q, k, v, attn_sink, l, bc, sc)
        ssd, s_ctx = _ssd(xbc, z, dt, zero_state, conv_w_p, conv_b_p, dtb_p, alog_p, dsk_p, snw_p, l, bc, sc)
        x1, h2, gates = _out_projection(attn, ssd, xp, w_out_b, n2, mods, wr_p, rb_col, l, ctx_row)
        xp = _moe(h2, gates, x1, mods, w_gate_b, w_up_b, w_down_b, fw, l, ctx_row, last)
        new_k.append(k.reshape(bc, sc, N_KV_HEADS, HEAD_DIM))
        new_v.append(v.reshape(bc, sc, N_KV_HEADS, HEAD_DIM))
        new_s.append(jnp.transpose(s_ctx.reshape(bc, 2, D_STATE, SSD_HEADS, SSD_HEADDIM), (0, 1, 3, 4, 2)))
        q, k, v, xbc, z, dt = _in_projection(xs, n1, mods, w_in_p, l, lat_row_proj, rope_tabs, sl)
        attn = _attention_latent(q, k, v, ck, cv, attn_sink, l, bl, sl)
        ssd, _ = _ssd(xbc, z, dt, st_in[:, l], conv_w_p, conv_b_p, dtb_p, alog_p, dsk_p, snw_p, l, bl, sl)
        x1, h2, gates = _out_projection(attn, ssd, xs, w_out_b, n2, mods, wr_p, rb_col, l, lat_row_proj)
        xs = _moe(h2, gates, x1, mods, w_gate_b, w_up_b, w_down_b, fw, l, lat_row_moe, last)
    return (xp.reshape(bc, sc, d), xs.reshape(bl, sl, d),
            jnp.stack(new_k, axis=1), jnp.stack(new_v, axis=1), jnp.stack(new_s, axis=1))
```

```python
import functools

import numpy as np
import jax
import jax.numpy as jnp
from jax import lax
from jax.experimental import pallas as pl
from jax.experimental.pallas import tpu as pltpu

F32 = jnp.float32
BF16 = jnp.bfloat16

D_MODEL = 1024
DEPTH = 4
GRID_W = 64
EPS = 1e-6
NEG_INF = -1e30
N_HEADS = 8
N_KV_HEADS = 2
GQA_GROUP = N_HEADS // N_KV_HEADS
HEAD_DIM = 64
ATTN_WIDTH = N_HEADS * HEAD_DIM
KV_WIDTH = N_KV_HEADS * HEAD_DIM
BLOCK = 128
ROPE_THETA = 10000.0
SSD_HEADS = 8
SSD_HEADDIM = 64
SSD_WIDTH = SSD_HEADS * SSD_HEADDIM
SSD_GROUPS = 2
D_STATE = 64
BC_WIDTH = SSD_GROUPS * D_STATE
D_CONV = 5
CHUNK = 128
XBC_WIDTH = SSD_WIDTH + 2 * BC_WIDTH
N_EXPERTS = 16
N_EXPERT_GROUPS = 4
EXPERTS_PER_GROUP = N_EXPERTS // N_EXPERT_GROUPS
D_FF = 256
GROUP_SHIFT = EXPERTS_PER_GROUP.bit_length() - 1
assert 1 << GROUP_SHIFT == EXPERTS_PER_GROUP

LANES = 128
SUBLANES = 8
VMEM_LIMIT = 56 * 1024 * 1024

O_Q = 0
O_K = O_Q + ATTN_WIDTH
O_V = O_K + KV_WIDTH
O_XBC = O_V + KV_WIDTH
O_Z = O_XBC + XBC_WIDTH
O_DT = O_Z + SSD_WIDTH
IN_PAD = O_DT + LANES
MOD_ROWS = 16
ROW_TILE = 1024
MOE_TILE = 512
MOE_BLOCK = 144
MOE_CHUNKS = MOE_TILE // MOE_BLOCK + N_EXPERT_GROUPS
MOE_ROWS = -(-MOE_CHUNKS * MOE_BLOCK // LANES) * LANES
GROUP_LANE = N_EXPERTS
GROUP_COUNT_SHIFT = N_EXPERT_GROUPS.bit_length() - 1
assert 1 << GROUP_COUNT_SHIFT == N_EXPERT_GROUPS
SSD_UNROLL = 4
LAT_QBLOCKS = 8
LOG2E = 1.4426950408889634


def _cparams(sem):
    return pltpu.CompilerParams(dimension_semantics=sem, vmem_limit_bytes=VMEM_LIMIT)


def _sigmoid(x):
    return 1.0 / (1.0 + jnp.exp(-x))


def _split2(x):
    hi = x.astype(BF16)
    lo = (x - hi.astype(F32)).astype(BF16)
    return hi, lo


def _split3(x):
    hi = x.astype(BF16)
    r = x - hi.astype(F32)
    mid = r.astype(BF16)
    lo = (r - mid.astype(F32)).astype(BF16)
    return hi, mid, lo


def _dot(a, b):
    return jnp.dot(a, b, preferred_element_type=F32)


def _dot_nt(a, b):
    return lax.dot_general(a, b, (((1,), (1,)), ((), ())), preferred_element_type=F32)


def _adaln_kernel(c_ref, w_ref, b_ref, o_ref):
    c = c_ref[...]
    s = c * _sigmoid(c)
    s_hi, s_lo = _split2(s)
    w = w_ref[...]
    w_hi, w_lo = _split2(w)
    acc = _dot(s_hi, w_hi) + _dot(s_lo, w_hi) + _dot(s_hi, w_lo)
    o_ref[...] = acc + b_ref[...]


def _adaln_all(cvec, w_ada, b_ada):
    depth, d, n = w_ada.shape
    tn = 1024
    return pl.pallas_call(
        _adaln_kernel,
        grid=(depth, n // tn),
        in_specs=[
            pl.BlockSpec((MOD_ROWS, d), lambda l, j: (0, 0)),
            pl.BlockSpec((None, d, tn), lambda l, j: (l, 0, j)),
            pl.BlockSpec((None, 1, tn), lambda l, j: (l, 0, j)),
        ],
        out_specs=pl.BlockSpec((None, MOD_ROWS, tn), lambda l, j: (l, 0, j)),
        out_shape=jax.ShapeDtypeStruct((depth, MOD_ROWS, n), F32),
        compiler_params=_cparams(("arbitrary", "arbitrary")),
        name="adaln",
    )(cvec, w_ada, b_ada.reshape(depth, 1, n))


def _mod_spec(layer, which, row_of_step):
    return pl.BlockSpec((None, None, None, 1, D_MODEL),
                        lambda i, *_: (layer, row_of_step(i), which, 0, 0))


def _rope(x, cos4, sin4):
    lane = lax.broadcasted_iota(jnp.int32, x.shape, 1)
    first_half = (lane & (HEAD_DIM - 1)) < (HEAD_DIM // 2)
    partner = jnp.where(first_half,
                        pltpu.roll(x, LANES - HEAD_DIM // 2, 1),
                        pltpu.roll(x, HEAD_DIM // 2, 1))
    return x * cos4 + partner * sin4


def _inproj_kernel(*refs, rope):
    if rope:
        (x_ref, nw_ref, sh_ref, sc_ref, w_ref, cos_ref, sin_ref,
         q_ref, k_ref, v_ref, xbc_ref, z_ref, dt_ref) = refs
    else:
        (x_ref, nw_ref, sh_ref, sc_ref, w_ref,
         q_ref, k_ref, v_ref, xbc_ref, z_ref, dt_ref) = refs
    x = x_ref[...]
    ms = jnp.mean(x * x, axis=-1, keepdims=True)
    y = x * lax.rsqrt(ms + EPS) * nw_ref[...]
    h = (y * (1.0 + sc_ref[...]) + sh_ref[...]).astype(BF16)

    def seg(a, b):
        return _dot(h, w_ref[:, a:b])

    qkv = seg(O_Q, O_XBC)
    q, k, v = qkv[:, O_Q:O_K], qkv[:, O_K:O_V], qkv[:, O_V:O_XBC]
    if rope:
        cos4 = cos_ref[...]
        sin4 = sin_ref[...]
        for j in range(ATTN_WIDTH // LANES):
            q_ref[:, j * LANES:(j + 1) * LANES] = _rope(
                q[:, j * LANES:(j + 1) * LANES], cos4, sin4).astype(q_ref.dtype)
        k_ref[...] = _rope(k, cos4, sin4)
    else:
        q_ref[...] = q.astype(q_ref.dtype)
        k_ref[...] = k
    v_ref[...] = v
    xbc_ref[...] = seg(O_XBC, O_Z)
    zdt = seg(O_Z, IN_PAD)
    z_ref[...] = zdt[:, 0:SSD_WIDTH]
    dt_ref[...] = zdt[:, SSD_WIDTH:]


def _in_projection(x, norm_w, mods, w_in_p, layer, row_of_step, rope_tabs, seq_len):
    t = x.shape[0]
    tm = ROW_TILE
    rope = rope_tabs is not None
    row = lambda i: (i, 0)
    in_specs = [
        pl.BlockSpec((tm, D_MODEL), row),
        pl.BlockSpec((None, 1, D_MODEL), lambda i: (layer, 0, 0)),
        _mod_spec(layer, 0, row_of_step),
        _mod_spec(layer, 1, row_of_step),
        pl.BlockSpec((None, D_MODEL, IN_PAD), lambda i: (layer, 0, 0)),
    ]
    args = [x, norm_w, mods, mods, w_in_p]
    if rope:
        steps_per_seq = seq_len // tm
        tab = pl.BlockSpec((tm, LANES), lambda i: (i % steps_per_seq, 0))
        in_specs += [tab, tab]
        args += list(rope_tabs)
    widths = (ATTN_WIDTH, KV_WIDTH, KV_WIDTH, XBC_WIDTH, SSD_WIDTH, LANES)
    dtypes = (BF16, F32, F32, F32, F32, F32)
    return pl.pallas_call(
        functools.partial(_inproj_kernel, rope=rope),
        grid=(t // tm,),
        in_specs=in_specs,
        out_specs=[pl.BlockSpec((tm, w), row) for w in widths],
        out_shape=[jax.ShapeDtypeStruct((t, w), dt) for w, dt in zip(widths, dtypes)],
        compiler_params=_cparams(("arbitrary",)),
        name="in_projection_rope" if rope else "in_projection",
    )(*args)


def _kv_operands(k, v):
    low = lax.broadcasted_iota(jnp.int32, k.shape, 1) < HEAD_DIM
    k_sw = pltpu.roll(k, HEAD_DIM, 1)
    v_sw = pltpu.roll(v, HEAD_DIM, 1)
    kdup = (jnp.where(low, k, k_sw).astype(BF16), jnp.where(low, k_sw, k).astype(BF16))
    vdup = (jnp.where(low, v, v_sw).astype(BF16), jnp.where(low, v_sw, v).astype(BF16))
    return kdup, vdup


def _attend_group(sink_ref, q_ref, o_ref, layer, g, r0, tq, segments):
    _attend_slabs(sink_ref, q_ref, o_ref, layer, g, range(GQA_GROUP // 2), r0, tq, segments)


def _attend_slabs(sink_ref, q_ref, o_ref, layer, g, slabs, r0, tq, segments):
    low = lax.broadcasted_iota(jnp.int32, (tq, LANES), 1) < HEAD_DIM
    base = g * GQA_GROUP * HEAD_DIM
    n_heads = 2 * len(slabs)
    rows = []
    for j in slabs:
        slab = q_ref[r0:r0 + tq, base + j * LANES:base + (j + 1) * LANES]
        zero = jnp.zeros_like(slab)
        rows += [jnp.where(low, slab, zero), jnp.where(low, zero, slab)]
    lhs = jnp.concatenate(rows, axis=0)
    sink = jnp.concatenate(
        [jnp.full((tq, LANES), sink_ref[layer, g * GQA_GROUP + 2 * j + i] * LOG2E, F32)
         for j in slabs for i in range(2)], axis=0)
    tiles = []
    for kdup, _, patches in segments:
        s = _dot_nt(lhs, kdup)
        ts = [s[:, c * LANES:(c + 1) * LANES] for c in range(s.shape[1] // LANES)]
        for c, mask in patches or ():
            ts[c] = (ts[c].reshape(n_heads, tq, LANES) + mask[None]).reshape(ts[c].shape)
        tiles.append(ts)
    flat = [t for ts in tiles for t in ts]
    m = jnp.max(functools.reduce(jnp.maximum, flat + [sink]), axis=-1, keepdims=True)
    probs = [[jnp.exp2(t - m) for t in ts] for ts in tiles]
    total = functools.reduce(jnp.add, [p for ps in probs for p in ps])
    lane0 = lax.broadcasted_iota(jnp.int32, sink.shape, 1) == 0
    total = total + jnp.where(lane0, jnp.exp2(sink - m), 0.0)
    denom = jnp.sum(total, axis=-1, keepdims=True)
    o = None
    for ps, (_, vdup, _) in zip(probs, segments):
        part = _dot(jnp.concatenate(ps, axis=-1).astype(BF16), vdup)
        o = part if o is None else o + part
    o = o * (1.0 / denom)
    for n, j in enumerate(slabs):
        pair = jnp.where(low, o[2 * n * tq:(2 * n + 1) * tq], o[(2 * n + 1) * tq:(2 * n + 2) * tq])
        o_ref[r0:r0 + tq, base + j * LANES:base + (j + 1) * LANES] = pair.astype(o_ref.dtype)


def _attn_ctx_kernel(sink_ref, q_ref, k_ref, v_ref, o_ref, *, layer):
    kdup, vdup = _kv_operands(k_ref[...], v_ref[...])
    for x in range(q_ref.shape[0] // BLOCK):
        for g in range(N_KV_HEADS):
            _attend_group(sink_ref, q_ref, o_ref, layer, g, x * BLOCK, BLOCK, [(kdup[g], vdup[g], None)])


def _attention_context(q, k, v, attn_sink, layer, batch, seq):
    row = lambda b: (b, 0)
    return pl.pallas_call(
        functools.partial(_attn_ctx_kernel, layer=layer),
        grid=(batch,),
        in_specs=[
            pl.BlockSpec(memory_space=pltpu.SMEM),
            pl.BlockSpec((seq, ATTN_WIDTH), row),
            pl.BlockSpec((seq, KV_WIDTH), row),
            pl.BlockSpec((seq, KV_WIDTH), row),
        ],
        out_specs=pl.BlockSpec((seq, ATTN_WIDTH), row),
        out_shape=jax.ShapeDtypeStruct((batch * seq, ATTN_WIDTH), BF16),
        compiler_params=_cparams(("arbitrary",)),
        name="attention_context",
    )(attn_sink, q, k, v)


def _attn_lat_kernel(sink_ref, q_ref, k_ref, v_ref, kc_ref, vc_ref, o_ref, *, layer, n_blocks):
    first_block = pl.program_id(1) * LAT_QBLOCKS
    band_blocks = LAT_QBLOCKS + 2

    def band(seq_ref):
        parts = []
        for i in range(band_blocks):
            blk = jnp.clip(first_block - 1 + i, 0, n_blocks - 1)
            parts.append(seq_ref[pl.ds(pl.multiple_of(blk * BLOCK, BLOCK), BLOCK), :])
        return jnp.concatenate(parts, axis=0)

    past_tiles = kc_ref.shape[0] // BLOCK
    kdup, vdup = _kv_operands(jnp.concatenate([kc_ref[...], band(k_ref)], axis=0),
                              jnp.concatenate([vc_ref[...], band(v_ref)], axis=0))
    qi = lax.broadcasted_iota(jnp.int32, (BLOCK, BLOCK), 0)
    kj = lax.broadcasted_iota(jnp.int32, (BLOCK, BLOCK), 1)
    for x in range(LAT_QBLOCKS):
        blk = first_block + x
        mask_prev = jnp.where(kj >= qi + jnp.where(blk > 0, 0, BLOCK), 0.0, NEG_INF)
        mask_next = jnp.where(kj <= qi - jnp.where(blk < n_blocks - 1, 0, BLOCK), 0.0, NEG_INF)
        patches = [(past_tiles, mask_prev), (past_tiles + 2, mask_next)]
        for g in range(N_KV_HEADS):
            if LAT_QBLOCKS == 1:
                keys, vals = kdup[g], vdup[g]
            else:
                pick = lambda a: jnp.concatenate(
                    [a[:past_tiles * BLOCK], a[(past_tiles + x) * BLOCK:(past_tiles + x + 3) * BLOCK]], axis=0)
                keys, vals = pick(kdup[g]), pick(vdup[g])
            _attend_group(sink_ref, q_ref, o_ref, layer, g, x * BLOCK, BLOCK, [(keys, vals, patches)])


def _attention_latent(q, k, v, cache_k, cache_v, attn_sink, layer, batch, seq):
    n_blocks = seq // BLOCK
    steps = n_blocks // LAT_QBLOCKS
    tq = LAT_QBLOCKS * BLOCK
    past = cache_k.shape[2]
    ctx = pl.BlockSpec((None, None, past, KV_WIDTH), lambda b, n: (b, layer, 0, 0))
    whole = pl.BlockSpec((seq, KV_WIDTH), lambda b, n: (b, 0))
    qo = pl.BlockSpec((tq, ATTN_WIDTH), lambda b, n: (b * steps + n, 0))
    return pl.pallas_call(
        functools.partial(_attn_lat_kernel, layer=layer, n_blocks=n_blocks),
        grid=(batch, steps),
        in_specs=[pl.BlockSpec(memory_space=pltpu.SMEM), qo, whole, whole, ctx, ctx],
        out_specs=qo,
        out_shape=jax.ShapeDtypeStruct((batch * seq, ATTN_WIDTH), BF16),
        compiler_params=_cparams(("arbitrary", "arbitrary")),
        name="attention_latent",
    )(attn_sink, q, k, v, cache_k, cache_v)


def _ssd_kernel(xbc_ref, z_ref, dt_ref, init_ref, cw_ref, cb_ref, dtb_ref, alog_ref, dsk_ref, nw_ref,
                tri_ref, exp_ref, y_ref, st_ref, pad_scr, act_scr, dtv_scr, cs_scr, sb_scr, sf_scr, sr_scr, *, seq):
    n_chunks = seq // CHUNK
    half = SSD_WIDTH // SSD_GROUPS
    pad_lo = SUBLANES
    pad_scr[0:pad_lo, :] = jnp.zeros((pad_lo, XBC_WIDTH), F32)
    pad_scr[pad_lo:pad_lo + seq, :] = xbc_ref[...]
    pad_scr[pad_lo + seq:2 * pad_lo + seq, :] = jnp.zeros((pad_lo, XBC_WIDTH), F32)

    lane = lax.broadcasted_iota(jnp.int32, (CHUNK, LANES), 1)
    is_fwd = lane < SSD_HEADS
    is_head = lane < 2 * SSD_HEADS
    ti = lax.broadcasted_iota(jnp.int32, (CHUNK, CHUNK), 0)
    tj = lax.broadcasted_iota(jnp.int32, (CHUNK, CHUNK), 1)
    lower = tj <= ti
    upper = tj >= ti
    a_row = jnp.where(is_head[0:1, :], -jnp.exp(alog_ref[...]), 0.0)

    def conv_silu(r0):
        win = pad_scr[pl.ds(r0, CHUNK + 2 * pad_lo), :]
        first = pad_lo - (D_CONV - 1) // 2
        acc = cb_ref[...] + cw_ref[0:1, :] * win[first:first + CHUNK, :]
        for kk in range(1, D_CONV):
            acc = acc + cw_ref[kk:kk + 1, :] * win[first + kk:first + kk + CHUNK, :]
        return acc * _sigmoid(acc)

    def step_sizes(r0):
        x = dt_ref[pl.ds(r0, CHUNK), :] + dtb_ref[...]
        dtv = jnp.maximum(x, 0.0) + jnp.log(1.0 + jnp.exp(-jnp.abs(x)))
        hi, mid, lo = _split3(dtv * a_row)
        tri = tri_ref[...]
        cs = _dot(tri, hi) + _dot(tri, mid) + _dot(tri, lo)
        dtv_scr[pl.ds(r0, CHUNK), :] = dtv
        cs_scr[pl.ds(r0, CHUNK), :] = cs
        return dtv, cs

    def prep(dtv, cs):
        ecs = cs - dtv * a_row
        tot = cs[CHUNK - 1:CHUNK, :]
        ldt = jnp.log(dtv)
        col_q = jnp.where(is_fwd, cs, ecs)
        row_q = jnp.where(is_fwd, cs - ldt, ecs + ldt).T
        scale_y = jnp.where(is_fwd, jnp.exp(cs), jnp.exp(tot - ecs))
        scale_s = dtv * jnp.where(is_fwd, jnp.exp(tot - cs), jnp.exp(ecs))
        ex = exp_ref[...]
        scale_y = _dot(scale_y.astype(BF16), ex)
        scale_s = _dot(scale_s.astype(BF16), ex)
        d_hi, d_mid, d_lo = _split3(jnp.broadcast_to(jnp.exp(tot), (SUBLANES, LANES)))
        decay = (_dot(d_hi, ex) + _dot(d_mid, ex) + _dot(d_lo, ex))[0:1, :]
        return col_q, row_q, scale_y, scale_s, decay

    def chunk_state(act, weights, decay, state):
        xw = (act[:, 0:SSD_WIDTH] * weights).astype(BF16)
        bt = act[:, SSD_WIDTH:SSD_WIDTH + BC_WIDTH].T.astype(BF16)
        upd = jnp.concatenate(
            [_dot(bt[g * D_STATE:(g + 1) * D_STATE, :], xw[:, g * half:(g + 1) * half])
             for g in range(SSD_GROUPS)], axis=-1)
        return decay * state + upd

    sr_scr[...] = init_ref[1]

    def bwd_body(i, carry):
        c = n_chunks - 1 - i
        r0 = pl.multiple_of(c * CHUNK, CHUNK)
        act = conv_silu(r0)
        act_scr[pl.ds(r0, CHUNK), :] = act
        _, _, _, scale_s, decay = prep(*step_sizes(r0))
        state = sr_scr[...]
        sb_scr[c] = state
        sr_scr[...] = chunk_state(act, scale_s[:, SSD_WIDTH:], decay[:, SSD_WIDTH:], state)
        return carry

    lax.fori_loop(0, n_chunks, bwd_body, 0, unroll=min(SSD_UNROLL, n_chunks))
    st_ref[1] = sr_scr[...]

    sf_scr[...] = init_ref[0]

    def fwd_body(c, carry):
        r0 = pl.multiple_of(c * CHUNK, CHUNK)
        act = act_scr[pl.ds(r0, CHUNK), :]
        col_q, row_q, scale_y, scale_s, decay = prep(dtv_scr[pl.ds(r0, CHUNK), :], cs_scr[pl.ds(r0, CHUNK), :])
        xs = act[:, 0:SSD_WIDTH]
        xs_b = xs.astype(BF16)
        bmat = act[:, SSD_WIDTH:SSD_WIDTH + BC_WIDTH].astype(BF16)
        cmat = act[:, SSD_WIDTH + BC_WIDTH:XBC_WIDTH].astype(BF16)
        s_f = sf_scr[...]
        s_b = sb_scr[c]
        y_parts = []
        for g in range(SSD_GROUPS):
            gs = slice(g * D_STATE, (g + 1) * D_STATE)
            cb = _dot_nt(cmat[:, gs], bmat[:, gs])
            for pair in range(SSD_HEADS // SSD_GROUPS // 2):
                ws = []
                for h in (g * 4 + 2 * pair, g * 4 + 2 * pair + 1):
                    hb = SSD_HEADS + h
                    e_f = jnp.exp(jnp.where(lower, col_q[:, h:h + 1] - row_q[h:h + 1, :], NEG_INF))
                    e_b = jnp.exp(jnp.where(upper, row_q[hb:hb + 1, :] - col_q[:, hb:hb + 1], NEG_INF))
                    ws.append((cb * (e_f + e_b)).astype(BF16))
                h0 = g * 4 + 2 * pair
                slab = xs_b[:, h0 * SSD_HEADDIM:(h0 + 2) * SSD_HEADDIM]
                first = lane < SSD_HEADDIM
                rhs = jnp.concatenate([jnp.where(first, slab, jnp.zeros_like(slab)),
                                       jnp.where(first, jnp.zeros_like(slab), slab)], axis=0)
                y_parts.append(_dot(jnp.concatenate(ws, axis=-1), rhs))
        y = jnp.concatenate(y_parts, axis=-1)
        off_f = jnp.concatenate(
            [_dot(cmat[:, g * D_STATE:(g + 1) * D_STATE], s_f[:, g * half:(g + 1) * half].astype(BF16))
             for g in range(SSD_GROUPS)], axis=-1)
        off_b = jnp.concatenate(
            [_dot(cmat[:, g * D_STATE:(g + 1) * D_STATE], s_b[:, g * half:(g + 1) * half].astype(BF16))
             for g in range(SSD_GROUPS)], axis=-1)
        y = y + off_f * scale_y[:, 0:SSD_WIDTH] + off_b * scale_y[:, SSD_WIDTH:] + dsk_ref[...] * xs
        zc = z_ref[pl.ds(r0, CHUNK), :]
        gated = y * (zc * _sigmoid(zc))
        outs = []
        for g in range(SSD_GROUPS):
            gg = gated[:, g * half:(g + 1) * half]
            outs.append(gg * lax.rsqrt(jnp.mean(gg * gg, axis=-1, keepdims=True) + EPS))
        y_ref[pl.ds(r0, CHUNK), :] = (jnp.concatenate(outs, axis=-1) * nw_ref[...]).astype(y_ref.dtype)
        sf_scr[...] = chunk_state(act, scale_s[:, 0:SSD_WIDTH], decay[:, 0:SSD_WIDTH], s_f)
        return carry

    lax.fori_loop(0, n_chunks, fwd_body, 0, unroll=min(SSD_UNROLL, n_chunks))
    st_ref[0] = sf_scr[...]


def _ssd_constants():
    r = np.arange(CHUNK)
    tri = (r[None, :] <= r[:, None]).astype(np.float32)
    cols = np.arange(2 * SSD_WIDTH)
    expand = (cols[None, :] // SSD_HEADDIM == np.arange(LANES)[:, None]).astype(np.float32)
    return jnp.asarray(tri, BF16), jnp.asarray(expand, BF16)


def _ssd(xbc, z, dt, init, conv_w, conv_b, dt_bias, a_log, d_skip, ssd_norm_w, layer, batch, seq):
    tri, expand = _ssd_constants()
    n_chunks = seq // CHUNK
    row = lambda b: (b, 0)
    lay = lambda b: (layer, 0, 0)
    const = lambda b: (0, 0)
    state_spec = pl.BlockSpec((None, 2, D_STATE, SSD_WIDTH), lambda b: (b, 0, 0, 0))
    return pl.pallas_call(
        functools.partial(_ssd_kernel, seq=seq),
        grid=(batch,),
        in_specs=[
            pl.BlockSpec((seq, XBC_WIDTH), row),
            pl.BlockSpec((seq, SSD_WIDTH), row),
            pl.BlockSpec((seq, LANES), row),
            state_spec,
            pl.BlockSpec((None, SUBLANES, XBC_WIDTH), lay),
            pl.BlockSpec((None, 1, XBC_WIDTH), lay),
            pl.BlockSpec((None, 1, LANES), lay),
            pl.BlockSpec((None, 1, LANES), lay),
            pl.BlockSpec((None, 1, SSD_WIDTH), lay),
            pl.BlockSpec((None, 1, SSD_WIDTH), lay),
            pl.BlockSpec((CHUNK, CHUNK), const),
            pl.BlockSpec((LANES, 2 * SSD_WIDTH), const),
        ],
        out_specs=[pl.BlockSpec((seq, SSD_WIDTH), row), state_spec],
        out_shape=[jax.ShapeDtypeStruct((batch * seq, SSD_WIDTH), BF16),
                   jax.ShapeDtypeStruct((batch, 2, D_STATE, SSD_WIDTH), F32)],
        scratch_shapes=[
            pltpu.VMEM((seq + 2 * SUBLANES, XBC_WIDTH), F32),
            pltpu.VMEM((seq, XBC_WIDTH), F32),
            pltpu.VMEM((seq, LANES), F32),
            pltpu.VMEM((seq, LANES), F32),
            pltpu.VMEM((n_chunks, D_STATE, SSD_WIDTH), F32),
            pltpu.VMEM((D_STATE, SSD_WIDTH), F32),
            pltpu.VMEM((D_STATE, SSD_WIDTH), F32),
        ],
        compiler_params=_cparams(("arbitrary",)),
        name="ssd",
    )(xbc, z, dt, init, conv_w, conv_b, dt_bias, a_log, d_skip, ssd_norm_w, tri, expand)


def _first_argmax(rows):
    best_v = rows[0]
    best_i = jnp.zeros(rows[0].shape, jnp.int32)
    for i in range(1, len(rows)):
        better = rows[i] > best_v
        best_v = jnp.where(better, rows[i], best_v)
        best_i = jnp.where(better, i, best_i)
    return best_i, best_v


def _outproj_kernel(attn_ref, ssd_ref, x_ref, w_ref, g1_ref, nw_ref, sh_ref, sc_ref, wr_ref, rb_ref,
                    x1_ref, h_ref, gates_ref):
    half = ATTN_WIDTH
    mix = _dot(attn_ref[...], w_ref[0:half, :]) + _dot(ssd_ref[...], w_ref[half:, :])
    x1 = x_ref[...] + g1_ref[...] * mix
    x1_ref[...] = x1
    ms = jnp.mean(x1 * x1, axis=-1, keepdims=True)
    y = x1 * lax.rsqrt(ms + EPS) * nw_ref[...]
    h = y * (1.0 + sc_ref[...]) + sh_ref[...]
    h_hi, h_lo = _split2(h)
    h_ref[...] = h_hi
    wr = wr_ref[...]
    l1 = _dot(h_hi, wr).T
    l2 = _dot(h_lo, wr).T
    ne = N_EXPERTS
    logits = l1[0:ne, :] + l1[ne:2 * ne, :] + l2[0:ne, :]
    scores = _sigmoid(logits)
    sel = scores + rb_ref[...]
    srow = [sel[e:e + 1, :] for e in range(ne)]
    group_score = []
    for g in range(N_EXPERT_GROUPS):
        r = srow[g * EXPERTS_PER_GROUP:(g + 1) * EXPERTS_PER_GROUP]
        pairs = [r[i] + r[j] for i in range(len(r)) for j in range(i + 1, len(r))]
        top2 = pairs[0]
        for p in pairs[1:]:
            top2 = jnp.maximum(top2, p)
        group_score.append(top2)
    best_group, _ = _first_argmax(group_score)
    eid = lax.broadcasted_iota(jnp.int32, sel.shape, 0)
    masked = jnp.where(lax.shift_right_logical(eid, GROUP_SHIFT) == best_group, sel, NEG_INF)
    i1, _ = _first_argmax([masked[e:e + 1, :] for e in range(ne)])
    masked2 = jnp.where(eid == i1, -jnp.inf, masked)
    i2, _ = _first_argmax([masked2[e:e + 1, :] for e in range(ne)])
    pick1 = eid == i1
    pick2 = eid == i2
    w1 = jnp.sum(jnp.where(pick1, scores, 0.0), axis=0, keepdims=True)
    w2 = jnp.sum(jnp.where(pick2, scores, 0.0), axis=0, keepdims=True)
    wsum = w1 + w2
    gates = jnp.where(pick1, w1 / wsum, 0.0) + jnp.where(pick2, w2 / wsum, 0.0)
    gid = lax.broadcasted_iota(jnp.int32, (N_EXPERT_GROUPS, gates.shape[1]), 0)
    onehot = jnp.where(gid == best_group, 1.0, 0.0)
    pad = jnp.zeros((LANES - ne - N_EXPERT_GROUPS, gates.shape[1]), F32)
    gates_ref[...] = jnp.concatenate([gates, onehot, pad], axis=0).T


def _out_projection(attn, ssd, x, w_out_b, norm_w, mods, wr_p, rb_col, layer, row_of_step):
    t = x.shape[0]
    tm = ROW_TILE
    row = lambda i: (i, 0)
    return pl.pallas_call(
        _outproj_kernel,
        grid=(t // tm,),
        in_specs=[
            pl.BlockSpec((tm, ATTN_WIDTH), row),
            pl.BlockSpec((tm, SSD_WIDTH), row),
            pl.BlockSpec((tm, D_MODEL), row),
            pl.BlockSpec((None, D_MODEL, D_MODEL), lambda i: (layer, 0, 0)),
            _mod_spec(layer, 2, row_of_step),
            pl.BlockSpec((None, 1, D_MODEL), lambda i: (layer, 0, 0)),
            _mod_spec(layer, 3, row_of_step),
            _mod_spec(layer, 4, row_of_step),
            pl.BlockSpec((D_MODEL, LANES), lambda i: (0, 0)),
            pl.BlockSpec((N_EXPERTS, 1), lambda i: (0, 0)),
        ],
        out_specs=[pl.BlockSpec((tm, D_MODEL), row), pl.BlockSpec((tm, D_MODEL), row),
                   pl.BlockSpec((tm, LANES), row)],
        out_shape=[jax.ShapeDtypeStruct((t, D_MODEL), F32), jax.ShapeDtypeStruct((t, D_MODEL), BF16),
                   jax.ShapeDtypeStruct((t, LANES), F32)],
        compiler_params=_cparams(("arbitrary",)),
        name="out_projection",
    )(attn, ssd, x, w_out_b, mods, norm_w, mods, mods, wr_p, rb_col)


def _moe_kernel(h_ref, gates_ref, x1_ref, g2_ref, wg_ref, wu_ref, wd_ref, fw_ref, tril_ref, triu_ref,
                o_ref, p_scr, acc_scr, seg_ref, *, final):
    tm = h_ref.shape[0]
    rows = p_scr.shape[0]
    g = gates_ref[...]
    lane = lax.broadcasted_iota(jnp.int32, g.shape, 1)
    onehot = jnp.where((lane >= GROUP_LANE) & (lane < GROUP_LANE + N_EXPERT_GROUPS), g, 0.0)
    cnt = _dot(tril_ref[...], onehot.astype(BF16))
    tot_i = cnt[tm - 1:tm, :].astype(jnp.int32)
    chunk0 = jnp.int32(0)
    offs = []
    for k in range(N_EXPERT_GROUPS):
        offs.append((chunk0 * MOE_BLOCK).astype(F32))
        chunk0 = chunk0 + (tot_i[0, GROUP_LANE + k] + (MOE_BLOCK - 1)) // MOE_BLOCK
        seg_ref[k] = chunk0
    n_chunks = chunk0

    lane1 = lax.broadcasted_iota(jnp.int32, (1, LANES), 1)
    off_row = functools.reduce(jnp.add, [jnp.where(lane1 == GROUP_LANE + k, offs[k], 0.0)
                                         for k in range(N_EXPERT_GROUPS)])
    pos_col = jnp.sum(onehot * (cnt + off_row - 1.0), axis=-1, keepdims=True)
    r_lane = lax.broadcasted_iota(jnp.int32, (tm, rows), 1).astype(F32)
    unsort = jnp.where(r_lane == pos_col, 1.0, 0.0).astype(BF16)
    onehot_t = g.T[GROUP_LANE:GROUP_LANE + SUBLANES, :]
    sub = lax.broadcasted_iota(jnp.int32, onehot_t.shape, 0)
    onehot_t = jnp.where(sub < N_EXPERT_GROUPS, onehot_t, 0.0)
    cnt_t = _dot(onehot_t.astype(BF16), triu_ref[...])
    off_t = functools.reduce(jnp.add, [jnp.where(sub == k, offs[k], 0.0) for k in range(N_EXPERT_GROUPS)])
    pos_row = jnp.sum(onehot_t * (cnt_t + off_t - 1.0), axis=0, keepdims=True)
    r_sub = lax.broadcasted_iota(jnp.int32, (rows, tm), 0).astype(F32)
    p_scr[...] = jnp.where(r_sub == pos_row, 1.0, 0.0).astype(BF16)
    acc_scr[...] = jnp.zeros(acc_scr.shape, acc_scr.dtype)
    g_hi, g_lo = _split2(g)

    def chunk(c, carry):
        @pl.when(c < n_chunks)
        def _():
            k = functools.reduce(jnp.add, [jnp.where(c >= seg_ref[i], 1, 0)
                                           for i in range(N_EXPERT_GROUPS - 1)])
            r0 = pl.multiple_of(c * MOE_BLOCK, 16)
            pc = p_scr[pl.ds(r0, MOE_BLOCK), :]
            xb = _dot(pc, h_ref[...]).astype(BF16)
            gsb = _dot(pc, g_hi) + _dot(pc, g_lo)
            lane_b = lax.broadcasted_iota(jnp.int32, gsb.shape, 1)
            y = None
            for j in range(EXPERTS_PER_GROUP):
                e = k * EXPERTS_PER_GROUP + j
                a = _dot(xb, wg_ref[e])
                u = _dot(xb, wu_ref[e])
                ge = jnp.sum(jnp.where(lane_b == e, gsb, 0.0), axis=-1, keepdims=True)
                part = _dot((a * _sigmoid(a) * u * ge).astype(BF16), wd_ref[e])
                y = part if y is None else y + part
            acc_scr[pl.ds(r0, MOE_BLOCK), :] = y.astype(acc_scr.dtype)

        return carry

    lax.fori_loop(0, MOE_CHUNKS, chunk, 0)

    x2 = x1_ref[...] + g2_ref[...] * _dot(unsort, acc_scr[...])
    if final:
        ms = jnp.mean(x2 * x2, axis=-1, keepdims=True)
        x2 = x2 * lax.rsqrt(ms + EPS) * fw_ref[...]
    o_ref[...] = x2


def _moe_constants(tm):
    r = np.arange(tm)
    tril = (r[None, :] <= r[:, None]).astype(np.float32)
    return jnp.asarray(tril, BF16), jnp.asarray(tril.T, BF16)


def _moe(h, gates, x1, mods, w_gate_b, w_up_b, w_down_b, final_w, layer, row_of_step, final):
    t = h.shape[0]
    tm = MOE_TILE
    tril, triu = _moe_constants(tm)
    row = lambda i: (i, 0)
    const = lambda i: (0, 0)
    resident = dict(pipeline_mode=pl.Buffered(1))
    return pl.pallas_call(
        functools.partial(_moe_kernel, final=final),
        grid=(t // tm,),
        in_specs=[
            pl.BlockSpec((tm, D_MODEL), row),
            pl.BlockSpec((tm, LANES), row),
            pl.BlockSpec((tm, D_MODEL), row),
            _mod_spec(layer, 5, row_of_step),
            pl.BlockSpec((None, N_EXPERTS, D_MODEL, D_FF), lambda i: (layer, 0, 0, 0), **resident),
            pl.BlockSpec((None, N_EXPERTS, D_MODEL, D_FF), lambda i: (layer, 0, 0, 0), **resident),
            pl.BlockSpec((None, N_EXPERTS, D_FF, D_MODEL), lambda i: (layer, 0, 0, 0), **resident),
            pl.BlockSpec((1, D_MODEL), const),
            pl.BlockSpec((tm, tm), const, **resident),
            pl.BlockSpec((tm, tm), const, **resident),
        ],
        out_specs=pl.BlockSpec((tm, D_MODEL), row),
        out_shape=jax.ShapeDtypeStruct((t, D_MODEL), F32),
        scratch_shapes=[
            pltpu.VMEM((MOE_ROWS, tm), BF16),
            pltpu.VMEM((MOE_ROWS, D_MODEL), BF16),
            pltpu.SMEM((N_EXPERT_GROUPS,), jnp.int32),
        ],
        compiler_params=_cparams(("arbitrary",)),
        name="moe_final" if final else "moe",
    )(h, gates, x1, mods, w_gate_b, w_up_b, w_down_b, final_w, tril, triu)


def _rope_tables(n_tokens):
    rows = n_tokens // GRID_W
    row = jnp.broadcast_to(jnp.arange(rows, dtype=F32)[:, None], (rows, GRID_W)).reshape(-1)
    col = jnp.broadcast_to(jnp.arange(GRID_W, dtype=F32)[None, :], (rows, GRID_W)).reshape(-1)
    n_freq = HEAD_DIM // 4
    inv_freq = jnp.power(ROPE_THETA, -jnp.arange(n_freq, dtype=F32) / n_freq)
    ang = jnp.concatenate([row[:, None] * inv_freq, col[:, None] * inv_freq], axis=-1)
    cos, sin = jnp.cos(ang), jnp.sin(ang)
    return jnp.tile(cos, (1, 4)), jnp.concatenate([-sin, sin, -sin, sin], axis=-1)


def kernel(x_prompt, x_sample, cache_k, cache_v, state_ssm, c, c_ctx, norm1_w, norm2_w, final_norm_w,
           w_ada, b_ada, w_in, conv_w, conv_b, attn_sink, dt_bias, a_log, d_skip, ssd_norm_w, w_out,
           w_router, router_bias, w_gate, w_up, w_down):
    bc, sc, d = x_prompt.shape
    bl, sl, _ = x_sample.shape
    depth = w_in.shape[0]
    assert d == D_MODEL and depth == DEPTH and bl + 1 <= MOD_ROWS
    assert sc % ROW_TILE == 0 or ROW_TILE % sc == 0
    assert sl % MOE_TILE == 0 and (bc * sc) % MOE_TILE == 0 and MOE_TILE % sc == 0

    cvec = jnp.zeros((MOD_ROWS, d), F32).at[0].set(c_ctx).at[1:1 + bl].set(c)
    mods = _adaln_all(cvec, w_ada, b_ada).reshape(depth, MOD_ROWS, 6, 1, d)

    scale = HEAD_DIM ** -0.5 * LOG2E
    w_in_p = jnp.concatenate(
        [w_in[:, :, :ATTN_WIDTH] * scale, w_in[:, :, ATTN_WIDTH:],
         jnp.zeros((depth, d, IN_PAD - w_in.shape[2]), F32)], axis=-1).astype(BF16)
    w_out_b = w_out.astype(BF16)
    w_gate_b, w_up_b, w_down_b = w_gate.astype(BF16), w_up.astype(BF16), w_down.astype(BF16)
    wr_hi = w_router.astype(BF16)
    wr_lo = (w_router - wr_hi.astype(F32)).astype(BF16)
    wr_p = jnp.concatenate([wr_hi, wr_lo, jnp.zeros((d, LANES - 2 * N_EXPERTS), BF16)], axis=-1)
    rb_col = router_bias.reshape(N_EXPERTS, 1)
    n1 = norm1_w.reshape(depth, 1, d)
    n2 = norm2_w.reshape(depth, 1, d)
    fw = final_norm_w.reshape(1, d)
    conv_w_p = jnp.concatenate([conv_w, jnp.zeros((depth, SUBLANES - D_CONV, XBC_WIDTH), F32)], axis=1)
    conv_b_p = conv_b.reshape(depth, 1, XBC_WIDTH)
    pad16 = lambda a: jnp.concatenate(
        [a.reshape(depth, 1, 2 * SSD_HEADS), jnp.zeros((depth, 1, LANES - 2 * SSD_HEADS), F32)], axis=-1)
    dtb_p = pad16(dt_bias)
    alog_p = pad16(a_log)
    dsk_p = jnp.repeat(d_skip, SSD_HEADDIM, axis=-1).reshape(depth, 1, SSD_WIDTH)
    snw_p = ssd_norm_w.reshape(depth, 1, SSD_WIDTH)
    rope_tabs = _rope_tables(sl)
    ck = cache_k.reshape(bl, depth, cache_k.shape[2], KV_WIDTH)
    cv = cache_v.reshape(bl, depth, cache_v.shape[2], KV_WIDTH)
    st_in = jnp.transpose(state_ssm, (0, 1, 2, 5, 3, 4)).reshape(bl, depth, 2, D_STATE, SSD_WIDTH)
    zero_state = jnp.zeros((bc, 2, D_STATE, SSD_WIDTH), F32)

    ctx_row = lambda i: 0
    lat_row_proj = lambda i: 1 + i // (sl // ROW_TILE)
    lat_row_moe = lambda i: 1 + i // (sl // MOE_TILE)

    xp = x_prompt.reshape(bc * sc, d)
    xs = x_sample.reshape(bl * sl, d)
    new_k, new_v, new_s = [], [], []
    for l in range(depth):
        last = l == depth - 1
        q, k, v, xbc, z, dt = _in_projection(xp, n1, mods, w_in_p, l, ctx_row, None, sc)
        attn = _attention_context(q, k, v, attn_sink, l, bc, sc)
        ssd, s_ctx = _ssd(xbc, z, dt, zero_state, conv_w_p, conv_b_p, dtb_p, alog_p, dsk_p, snw_p, l, bc, sc)
        x1, h2, gates = _out_projection(attn, ssd, xp, w_out_b, n2, mods, wr_p, rb_col, l, ctx_row)
        xp = _moe(h2, gates, x1, mods, w_gate_b, w_up_b, w_down_b, fw, l, ctx_row, last)
        new_k.append(k.reshape(bc, sc, N_KV_HEADS, HEAD_DIM))
        new_v.append(v.reshape(bc, sc, N_KV_HEADS, HEAD_DIM))
        new_s.append(jnp.transpose(s_ctx.reshape(bc, 2, D_STATE, SSD_HEADS, SSD_HEADDIM), (0, 1, 3, 4, 2)))
        q, k, v, xbc, z, dt = _in_projection(xs, n1, mods, w_in_p, l, lat_row_proj, rope_tabs, sl)
        attn = _attention_latent(q, k, v, ck, cv, attn_sink, l, bl, sl)
        ssd, _ = _ssd(xbc, z, dt, st_in[:, l], conv_w_p, conv_b_p, dtb_p, alog_p, dsk_p, snw_p, l, bl, sl)
        x1, h2, gates = _out_projection(attn, ssd, xs, w_out_b, n2, mods, wr_p, rb_col, l, lat_row_proj)
        xs = _moe(h2, gates, x1, mods, w_gate_b, w_up_b, w_down_b, fw, l, lat_row_moe, last)
    return (xp.reshape(bc, sc, d), xs.reshape(bl, sl, d),
            jnp.stack(new_k, axis=1), jnp.stack(new_v, axis=1), jnp.stack(new_s, axis=1))
```

```python
import functools

import numpy as np
import jax
import jax.numpy as jnp
from jax import lax
from jax.experimental import pallas as pl
from jax.experimental.pallas import tpu as pltpu

F32 = jnp.float32
BF16 = jnp.bfloat16

D_MODEL = 1024
DEPTH = 4
GRID_W = 64
EPS = 1e-6
NEG_INF = -1e30
N_HEADS = 8
N_KV_HEADS = 2
GQA_GROUP = N_HEADS // N_KV_HEADS
HEAD_DIM = 64
ATTN_WIDTH = N_HEADS * HEAD_DIM
KV_WIDTH = N_KV_HEADS * HEAD_DIM
BLOCK = 128
ROPE_THETA = 10000.0
SSD_HEADS = 8
SSD_HEADDIM = 64
SSD_WIDTH = SSD_HEADS * SSD_HEADDIM
SSD_GROUPS = 2
D_STATE = 64
BC_WIDTH = SSD_GROUPS * D_STATE
D_CONV = 5
CHUNK = 128
XBC_WIDTH = SSD_WIDTH + 2 * BC_WIDTH
N_EXPERTS = 16
N_EXPERT_GROUPS = 4
EXPERTS_PER_GROUP = N_EXPERTS // N_EXPERT_GROUPS
D_FF = 256
GROUP_SHIFT = EXPERTS_PER_GROUP.bit_length() - 1
assert 1 << GROUP_SHIFT == EXPERTS_PER_GROUP

LANES = 128
SUBLANES = 8
VMEM_LIMIT = 56 * 1024 * 1024

O_Q = 0
O_K = O_Q + ATTN_WIDTH
O_V = O_K + KV_WIDTH
O_XBC = O_V + KV_WIDTH
O_Z = O_XBC + XBC_WIDTH
O_DT = O_Z + SSD_WIDTH
IN_PAD = O_DT + LANES
MOD_ROWS = 16
ROW_TILE = 1024
MOE_TILE = 512
MOE_BLOCK = 144
MOE_CHUNKS = MOE_TILE // MOE_BLOCK + N_EXPERT_GROUPS
MOE_ROWS = -(-MOE_CHUNKS * MOE_BLOCK // LANES) * LANES
GROUP_LANE = N_EXPERTS
GROUP_COUNT_SHIFT = N_EXPERT_GROUPS.bit_length() - 1
assert 1 << GROUP_COUNT_SHIFT == N_EXPERT_GROUPS
SSD_UNROLL = 4
LAT_QBLOCKS = 8
LOG2E = 1.4426950408889634


def _cparams(sem):
    return pltpu.CompilerParams(dimension_semantics=sem, vmem_limit_bytes=VMEM_LIMIT)


def _sigmoid(x):
    return 1.0 / (1.0 + jnp.exp(-x))


def _split2(x):
    hi = x.astype(BF16)
    lo = (x - hi.astype(F32)).astype(BF16)
    return hi, lo


def _split3(x):
    hi = x.astype(BF16)
    r = x - hi.astype(F32)
    mid = r.astype(BF16)
    lo = (r - mid.astype(F32)).astype(BF16)
    return hi, mid, lo


def _dot(a, b):
    return jnp.dot(a, b, preferred_element_type=F32)


def _dot_nt(a, b):
    return lax.dot_general(a, b, (((1,), (1,)), ((), ())), preferred_element_type=F32)


def _adaln_kernel(c_ref, w_ref, b_ref, o_ref):
    c = c_ref[...]
    s = c * _sigmoid(c)
    s_hi, s_lo = _split2(s)
    w = w_ref[...]
    w_hi, w_lo = _split2(w)
    acc = _dot(s_hi, w_hi) + _dot(s_lo, w_hi) + _dot(s_hi, w_lo)
    o_ref[...] = acc + b_ref[...]


def _adaln_all(cvec, w_ada, b_ada):
    depth, d, n = w_ada.shape
    tn = 1024
    return pl.pallas_call(
        _adaln_kernel,
        grid=(depth, n // tn),
        in_specs=[
            pl.BlockSpec((MOD_ROWS, d), lambda l, j: (0, 0)),
            pl.BlockSpec((None, d, tn), lambda l, j: (l, 0, j)),
            pl.BlockSpec((None, 1, tn), lambda l, j: (l, 0, j)),
        ],
        out_specs=pl.BlockSpec((None, MOD_ROWS, tn), lambda l, j: (l, 0, j)),
        out_shape=jax.ShapeDtypeStruct((depth, MOD_ROWS, n), F32),
        compiler_params=_cparams(("arbitrary", "arbitrary")),
        name="adaln",
    )(cvec, w_ada, b_ada.reshape(depth, 1, n))


def _mod_spec(layer, which, row_of_step):
    return pl.BlockSpec((None, None, None, 1, D_MODEL),
                        lambda i, *_: (layer, row_of_step(i), which, 0, 0))


def _rope(x, cos4, sin4):
    lane = lax.broadcasted_iota(jnp.int32, x.shape, 1)
    first_half = (lane & (HEAD_DIM - 1)) < (HEAD_DIM // 2)
    partner = jnp.where(first_half,
                        pltpu.roll(x, LANES - HEAD_DIM // 2, 1),
                        pltpu.roll(x, HEAD_DIM // 2, 1))
    return x * cos4 + partner * sin4


def _inproj_kernel(*refs, rope):
    if rope:
        (x_ref, nw_ref, sh_ref, sc_ref, w_ref, cos_ref, sin_ref,
         q_ref, k_ref, v_ref, xbc_ref, z_ref, dt_ref, wb_scr) = refs
    else:
        (x_ref, nw_ref, sh_ref, sc_ref, w_ref,
         q_ref, k_ref, v_ref, xbc_ref, z_ref, dt_ref, wb_scr) = refs

    @pl.when(pl.program_id(0) == 0)
    def _():
        w = w_ref[...]
        wb_scr[:, O_Q:O_K] = (w[:, O_Q:O_K] * (HEAD_DIM ** -0.5 * LOG2E)).astype(BF16)
        wb_scr[:, O_K:O_DT] = w[:, O_K:O_DT].astype(BF16)
        pad = jnp.zeros((w.shape[0], IN_PAD - w.shape[1]), F32)
        wb_scr[:, O_DT:IN_PAD] = jnp.concatenate([w[:, O_DT:], pad], axis=-1).astype(BF16)

    x = x_ref[...]
    ms = jnp.mean(x * x, axis=-1, keepdims=True)
    y = x * lax.rsqrt(ms + EPS) * nw_ref[...]
    h = (y * (1.0 + sc_ref[...]) + sh_ref[...]).astype(BF16)

    def seg(a, b):
        return _dot(h, wb_scr[:, a:b])

    qkv = seg(O_Q, O_XBC)
    q, k, v = qkv[:, O_Q:O_K], qkv[:, O_K:O_V], qkv[:, O_V:O_XBC]
    if rope:
        cos4 = cos_ref[...]
        sin4 = sin_ref[...]
        for j in range(ATTN_WIDTH // LANES):
            q_ref[:, j * LANES:(j + 1) * LANES] = _rope(
                q[:, j * LANES:(j + 1) * LANES], cos4, sin4).astype(q_ref.dtype)
        k_ref[...] = _rope(k, cos4, sin4)
    else:
        q_ref[...] = q.astype(q_ref.dtype)
        k_ref[...] = k
    v_ref[...] = v
    xbc_ref[...] = seg(O_XBC, O_Z)
    zdt = seg(O_Z, IN_PAD)
    z_ref[...] = zdt[:, 0:SSD_WIDTH]
    dt_ref[...] = zdt[:, SSD_WIDTH:]


def _in_projection(x, norm_w, mods, w_in, layer, row_of_step, rope_tabs, seq_len):
    t = x.shape[0]
    tm = ROW_TILE
    rope = rope_tabs is not None
    row = lambda i: (i, 0)
    in_specs = [
        pl.BlockSpec((tm, D_MODEL), row),
        pl.BlockSpec((None, 1, D_MODEL), lambda i: (layer, 0, 0)),
        _mod_spec(layer, 0, row_of_step),
        _mod_spec(layer, 1, row_of_step),
        pl.BlockSpec((None, D_MODEL, w_in.shape[2]), lambda i: (layer, 0, 0), pipeline_mode=pl.Buffered(1)),
    ]
    args = [x, norm_w, mods, mods, w_in]
    if rope:
        steps_per_seq = seq_len // tm
        tab = pl.BlockSpec((tm, LANES), lambda i: (i % steps_per_seq, 0))
        in_specs += [tab, tab]
        args += list(rope_tabs)
    widths = (ATTN_WIDTH, KV_WIDTH, KV_WIDTH, XBC_WIDTH, SSD_WIDTH, LANES)
    dtypes = (BF16, F32, F32, F32, F32, F32)
    return pl.pallas_call(
        functools.partial(_inproj_kernel, rope=rope),
        grid=(t // tm,),
        in_specs=in_specs,
        out_specs=[pl.BlockSpec((tm, w), row) for w in widths],
        out_shape=[jax.ShapeDtypeStruct((t, w), dt) for w, dt in zip(widths, dtypes)],
        scratch_shapes=[pltpu.VMEM((D_MODEL, IN_PAD), BF16)],
        compiler_params=_cparams(("arbitrary",)),
        name="in_projection_rope" if rope else "in_projection",
    )(*args)


def _kv_operands(k, v):
    low = lax.broadcasted_iota(jnp.int32, k.shape, 1) < HEAD_DIM
    k_sw = pltpu.roll(k, HEAD_DIM, 1)
    v_sw = pltpu.roll(v, HEAD_DIM, 1)
    kdup = (jnp.where(low, k, k_sw).astype(BF16), jnp.where(low, k_sw, k).astype(BF16))
    vdup = (jnp.where(low, v, v_sw).astype(BF16), jnp.where(low, v_sw, v).astype(BF16))
    return kdup, vdup


def _attend_group(sink_ref, q_ref, o_ref, layer, g, r0, tq, segments):
    _attend_slabs(sink_ref, q_ref, o_ref, layer, g, range(GQA_GROUP // 2), r0, tq, segments)


def _attend_slabs(sink_ref, q_ref, o_ref, layer, g, slabs, r0, tq, segments):
    low = lax.broadcasted_iota(jnp.int32, (tq, LANES), 1) < HEAD_DIM
    base = g * GQA_GROUP * HEAD_DIM
    n_heads = 2 * len(slabs)
    rows = []
    for j in slabs:
        slab = q_ref[r0:r0 + tq, base + j * LANES:base + (j + 1) * LANES]
        zero = jnp.zeros_like(slab)
        rows += [jnp.where(low, slab, zero), jnp.where(low, zero, slab)]
    lhs = jnp.concatenate(rows, axis=0)
    sink = jnp.concatenate(
        [jnp.full((tq, LANES), sink_ref[layer, g * GQA_GROUP + 2 * j + i] * LOG2E, F32)
         for j in slabs for i in range(2)], axis=0)
    tiles = []
    for kdup, _, patches in segments:
        s = _dot_nt(lhs, kdup)
        ts = [s[:, c * LANES:(c + 1) * LANES] for c in range(s.shape[1] // LANES)]
        for c, mask in patches or ():
            ts[c] = (ts[c].reshape(n_heads, tq, LANES) + mask[None]).reshape(ts[c].shape)
        tiles.append(ts)
    flat = [t for ts in tiles for t in ts]
    m = jnp.max(functools.reduce(jnp.maximum, flat + [sink]), axis=-1, keepdims=True)
    probs = [[jnp.exp2(t - m) for t in ts] for ts in tiles]
    total = functools.reduce(jnp.add, [p for ps in probs for p in ps])
    lane0 = lax.broadcasted_iota(jnp.int32, sink.shape, 1) == 0
    total = total + jnp.where(lane0, jnp.exp2(sink - m), 0.0)
    denom = jnp.sum(total, axis=-1, keepdims=True)
    o = None
    for ps, (_, vdup, _) in zip(probs, segments):
        part = _dot(jnp.concatenate(ps, axis=-1).astype(BF16), vdup)
        o = part if o is None else o + part
    o = o * (1.0 / denom)
    for n, j in enumerate(slabs):
        pair = jnp.where(low, o[2 * n * tq:(2 * n + 1) * tq], o[(2 * n + 1) * tq:(2 * n + 2) * tq])
        o_ref[r0:r0 + tq, base + j * LANES:base + (j + 1) * LANES] = pair.astype(o_ref.dtype)


def _attn_ctx_kernel(sink_ref, q_ref, k_ref, v_ref, o_ref, *, layer):
    kdup, vdup = _kv_operands(k_ref[...], v_ref[...])
    for x in range(q_ref.shape[0] // BLOCK):
        for g in range(N_KV_HEADS):
            _attend_group(sink_ref, q_ref, o_ref, layer, g, x * BLOCK, BLOCK, [(kdup[g], vdup[g], None)])


def _attention_context(q, k, v, attn_sink, layer, batch, seq):
    row = lambda b: (b, 0)
    return pl.pallas_call(
        functools.partial(_attn_ctx_kernel, layer=layer),
        grid=(batch,),
        in_specs=[
            pl.BlockSpec(memory_space=pltpu.SMEM),
            pl.BlockSpec((seq, ATTN_WIDTH), row),
            pl.BlockSpec((seq, KV_WIDTH), row),
            pl.BlockSpec((seq, KV_WIDTH), row),
        ],
        out_specs=pl.BlockSpec((seq, ATTN_WIDTH), row),
        out_shape=jax.ShapeDtypeStruct((batch * seq, ATTN_WIDTH), BF16),
        compiler_params=_cparams(("arbitrary",)),
        name="attention_context",
    )(attn_sink, q, k, v)


def _attn_lat_kernel(sink_ref, q_ref, k_ref, v_ref, kc_ref, vc_ref, o_ref, *, layer, n_blocks):
    first_block = pl.program_id(1) * LAT_QBLOCKS
    band_blocks = LAT_QBLOCKS + 2

    def band(seq_ref):
        parts = []
        for i in range(band_blocks):
            blk = jnp.clip(first_block - 1 + i, 0, n_blocks - 1)
            parts.append(seq_ref[pl.ds(pl.multiple_of(blk * BLOCK, BLOCK), BLOCK), :])
        return jnp.concatenate(parts, axis=0)

    past_tiles = kc_ref.shape[0] // BLOCK
    kdup, vdup = _kv_operands(jnp.concatenate([kc_ref[...], band(k_ref)], axis=0),
                              jnp.concatenate([vc_ref[...], band(v_ref)], axis=0))
    qi = lax.broadcasted_iota(jnp.int32, (BLOCK, BLOCK), 0)
    kj = lax.broadcasted_iota(jnp.int32, (BLOCK, BLOCK), 1)
    for x in range(LAT_QBLOCKS):
        blk = first_block + x
        mask_prev = jnp.where(kj >= qi + jnp.where(blk > 0, 0, BLOCK), 0.0, NEG_INF)
        mask_next = jnp.where(kj <= qi - jnp.where(blk < n_blocks - 1, 0, BLOCK), 0.0, NEG_INF)
        patches = [(past_tiles, mask_prev), (past_tiles + 2, mask_next)]
        for g in range(N_KV_HEADS):
            if LAT_QBLOCKS == 1:
                keys, vals = kdup[g], vdup[g]
            else:
                pick = lambda a: jnp.concatenate(
                    [a[:past_tiles * BLOCK], a[(past_tiles + x) * BLOCK:(past_tiles + x + 3) * BLOCK]], axis=0)
                keys, vals = pick(kdup[g]), pick(vdup[g])
            _attend_group(sink_ref, q_ref, o_ref, layer, g, x * BLOCK, BLOCK, [(keys, vals, patches)])


def _attention_latent(q, k, v, cache_k, cache_v, attn_sink, layer, batch, seq):
    n_blocks = seq // BLOCK
    steps = n_blocks // LAT_QBLOCKS
    tq = LAT_QBLOCKS * BLOCK
    past = cache_k.shape[2]
    ctx = pl.BlockSpec((None, None, past, KV_WIDTH), lambda b, n: (b, layer, 0, 0))
    whole = pl.BlockSpec((seq, KV_WIDTH), lambda b, n: (b, 0))
    qo = pl.BlockSpec((tq, ATTN_WIDTH), lambda b, n: (b * steps + n, 0))
    return pl.pallas_call(
        functools.partial(_attn_lat_kernel, layer=layer, n_blocks=n_blocks),
        grid=(batch, steps),
        in_specs=[pl.BlockSpec(memory_space=pltpu.SMEM), qo, whole, whole, ctx, ctx],
        out_specs=qo,
        out_shape=jax.ShapeDtypeStruct((batch * seq, ATTN_WIDTH), BF16),
        compiler_params=_cparams(("arbitrary", "arbitrary")),
        name="attention_latent",
    )(attn_sink, q, k, v, cache_k, cache_v)


def _ssd_kernel(xbc_ref, z_ref, dt_ref, init_ref, cw_ref, cb_ref, dtb_ref, alog_ref, dsk_ref, nw_ref,
                tri_ref, exp_ref, y_ref, st_ref, pad_scr, act_scr, dtv_scr, cs_scr, sb_scr, sf_scr, sr_scr, *, seq):
    n_chunks = seq // CHUNK
    half = SSD_WIDTH // SSD_GROUPS
    pad_lo = SUBLANES
    pad_scr[0:pad_lo, :] = jnp.zeros((pad_lo, XBC_WIDTH), F32)
    pad_scr[pad_lo:pad_lo + seq, :] = xbc_ref[...]
    pad_scr[pad_lo + seq:2 * pad_lo + seq, :] = jnp.zeros((pad_lo, XBC_WIDTH), F32)

    lane = lax.broadcasted_iota(jnp.int32, (CHUNK, LANES), 1)
    is_fwd = lane < SSD_HEADS
    is_head = lane < 2 * SSD_HEADS
    ti = lax.broadcasted_iota(jnp.int32, (CHUNK, CHUNK), 0)
    tj = lax.broadcasted_iota(jnp.int32, (CHUNK, CHUNK), 1)
    lower = tj <= ti
    upper = tj >= ti
    a_row = jnp.where(is_head[0:1, :], -jnp.exp(alog_ref[...]), 0.0)

    def conv_silu(r0):
        win = pad_scr[pl.ds(r0, CHUNK + 2 * pad_lo), :]
        first = pad_lo - (D_CONV - 1) // 2
        n_win = CHUNK + 2 * pad_lo
        tap = lambda kk: pltpu.roll(win, (n_win - first - kk) % n_win, 0)[0:CHUNK, :]
        acc = cb_ref[...] + cw_ref[0:1, :] * tap(0)
        for kk in range(1, D_CONV):
            acc = acc + cw_ref[kk:kk + 1, :] * tap(kk)
        return acc * _sigmoid(acc)

    def step_sizes(r0):
        x = dt_ref[pl.ds(r0, CHUNK), :] + dtb_ref[...]
        dtv = jnp.maximum(x, 0.0) + jnp.log(1.0 + jnp.exp(-jnp.abs(x)))
        hi, mid, lo = _split3(dtv * a_row)
        tri = tri_ref[...]
        cs = _dot(tri, hi) + _dot(tri, mid) + _dot(tri, lo)
        dtv_scr[pl.ds(r0, CHUNK), :] = dtv
        cs_scr[pl.ds(r0, CHUNK), :] = cs
        return dtv, cs

    def prep(dtv, cs):
        ecs = cs - dtv * a_row
        tot = cs[CHUNK - 1:CHUNK, :]
        ldt = jnp.log(dtv)
        col_q = jnp.where(is_fwd, cs, ecs)
        row_q = jnp.where(is_fwd, cs - ldt, ecs + ldt).T
        scale_y = jnp.where(is_fwd, jnp.exp(cs), jnp.exp(tot - ecs))
        scale_s = dtv * jnp.where(is_fwd, jnp.exp(tot - cs), jnp.exp(ecs))
        ex = exp_ref[...]
        scale_y = _dot(scale_y.astype(BF16), ex)
        scale_s = _dot(scale_s.astype(BF16), ex)
        d_hi, d_mid, d_lo = _split3(jnp.broadcast_to(jnp.exp(tot), (SUBLANES, LANES)))
        decay = (_dot(d_hi, ex) + _dot(d_mid, ex) + _dot(d_lo, ex))[0:1, :]
        return col_q, row_q, scale_y, scale_s, decay

    def chunk_state(act, weights, decay, state):
        xw = (act[:, 0:SSD_WIDTH] * weights).astype(BF16)
        bt = act[:, SSD_WIDTH:SSD_WIDTH + BC_WIDTH].T.astype(BF16)
        upd = jnp.concatenate(
            [_dot(bt[g * D_STATE:(g + 1) * D_STATE, :], xw[:, g * half:(g + 1) * half])
             for g in range(SSD_GROUPS)], axis=-1)
        return decay * state + upd

    sr_scr[...] = init_ref[1]

    def bwd_body(i, carry):
        c = n_chunks - 1 - i
        r0 = pl.multiple_of(c * CHUNK, CHUNK)
        act = conv_silu(r0)
        act_scr[pl.ds(r0, CHUNK), :] = act
        _, _, _, scale_s, decay = prep(*step_sizes(r0))
        state = sr_scr[...]
        sb_scr[c] = state
        sr_scr[...] = chunk_state(act, scale_s[:, SSD_WIDTH:], decay[:, SSD_WIDTH:], state)
        return carry

    lax.fori_loop(0, n_chunks, bwd_body, 0, unroll=min(SSD_UNROLL, n_chunks))
    st_ref[1] = sr_scr[...]

    sf_scr[...] = init_ref[0]

    def fwd_body(c, carry):
        r0 = pl.multiple_of(c * CHUNK, CHUNK)
        act = act_scr[pl.ds(r0, CHUNK), :]
        col_q, row_q, scale_y, scale_s, decay = prep(dtv_scr[pl.ds(r0, CHUNK), :], cs_scr[pl.ds(r0, CHUNK), :])
        xs = act[:, 0:SSD_WIDTH]
        xs_b = xs.astype(BF16)
        bmat = act[:, SSD_WIDTH:SSD_WIDTH + BC_WIDTH].astype(BF16)
        cmat = act[:, SSD_WIDTH + BC_WIDTH:XBC_WIDTH].astype(BF16)
        s_f = sf_scr[...]
        s_b = sb_scr[c]
        y_parts = []
        for g in range(SSD_GROUPS):
            gs = slice(g * D_STATE, (g + 1) * D_STATE)
            cb = _dot_nt(cmat[:, gs], bmat[:, gs])
            for pair in range(SSD_HEADS // SSD_GROUPS // 2):
                ws = []
                for h in (g * 4 + 2 * pair, g * 4 + 2 * pair + 1):
                    hb = SSD_HEADS + h
                    e_f = jnp.exp(jnp.where(lower, col_q[:, h:h + 1] - row_q[h:h + 1, :], NEG_INF))
                    e_b = jnp.exp(jnp.where(upper, row_q[hb:hb + 1, :] - col_q[:, hb:hb + 1], NEG_INF))
                    ws.append((cb * (e_f + e_b)).astype(BF16))
                h0 = g * 4 + 2 * pair
                slab = xs_b[:, h0 * SSD_HEADDIM:(h0 + 2) * SSD_HEADDIM]
                first = lane < SSD_HEADDIM
                rhs = jnp.concatenate([jnp.where(first, slab, jnp.zeros_like(slab)),
                                       jnp.where(first, jnp.zeros_like(slab), slab)], axis=0)
                y_parts.append(_dot(jnp.concatenate(ws, axis=-1), rhs))
        y = jnp.concatenate(y_parts, axis=-1)
        off_f = jnp.concatenate(
            [_dot(cmat[:, g * D_STATE:(g + 1) * D_STATE], s_f[:, g * half:(g + 1) * half].astype(BF16))
             for g in range(SSD_GROUPS)], axis=-1)
        off_b = jnp.concatenate(
            [_dot(cmat[:, g * D_STATE:(g + 1) * D_STATE], s_b[:, g * half:(g + 1) * half].astype(BF16))
             for g in range(SSD_GROUPS)], axis=-1)
        y = y + off_f * scale_y[:, 0:SSD_WIDTH] + off_b * scale_y[:, SSD_WIDTH:] + dsk_ref[...] * xs
        zc = z_ref[pl.ds(r0, CHUNK), :]
        gated = y * (zc * _sigmoid(zc))
        outs = []
        for g in range(SSD_GROUPS):
            gg = gated[:, g * half:(g + 1) * half]
            outs.append(gg * lax.rsqrt(jnp.mean(gg * gg, axis=-1, keepdims=True) + EPS))
        y_ref[pl.ds(r0, CHUNK), :] = (jnp.concatenate(outs, axis=-1) * nw_ref[...]).astype(y_ref.dtype)
        sf_scr[...] = chunk_state(act, scale_s[:, 0:SSD_WIDTH], decay[:, 0:SSD_WIDTH], s_f)
        return carry

    lax.fori_loop(0, n_chunks, fwd_body, 0, unroll=min(SSD_UNROLL, n_chunks))
    st_ref[0] = sf_scr[...]


def _ssd_constants():
    r = np.arange(CHUNK)
    tri = (r[None, :] <= r[:, None]).astype(np.float32)
    cols = np.arange(2 * SSD_WIDTH)
    expand = (cols[None, :] // SSD_HEADDIM == np.arange(LANES)[:, None]).astype(np.float32)
    return jnp.asarray(tri, BF16), jnp.asarray(expand, BF16)


def _ssd(xbc, z, dt, init, conv_w, conv_b, dt_bias, a_log, d_skip, ssd_norm_w, layer, batch, seq):
    tri, expand = _ssd_constants()
    n_chunks = seq // CHUNK
    row = lambda b: (b, 0)
    lay = lambda b: (layer, 0, 0)
    const = lambda b: (0, 0)
    state_spec = pl.BlockSpec((None, 2, D_STATE, SSD_WIDTH), lambda b: (b, 0, 0, 0))
    return pl.pallas_call(
        functools.partial(_ssd_kernel, seq=seq),
        grid=(batch,),
        in_specs=[
            pl.BlockSpec((seq, XBC_WIDTH), row),
            pl.BlockSpec((seq, SSD_WIDTH), row),
            pl.BlockSpec((seq, LANES), row),
            state_spec,
            pl.BlockSpec((None, SUBLANES, XBC_WIDTH), lay),
            pl.BlockSpec((None, 1, XBC_WIDTH), lay),
            pl.BlockSpec((None, 1, LANES), lay),
            pl.BlockSpec((None, 1, LANES), lay),
            pl.BlockSpec((None, 1, SSD_WIDTH), lay),
            pl.BlockSpec((None, 1, SSD_WIDTH), lay),
            pl.BlockSpec((CHUNK, CHUNK), const),
            pl.BlockSpec((LANES, 2 * SSD_WIDTH), const),
        ],
        out_specs=[pl.BlockSpec((seq, SSD_WIDTH), row), state_spec],
        out_shape=[jax.ShapeDtypeStruct((batch * seq, SSD_WIDTH), BF16),
                   jax.ShapeDtypeStruct((batch, 2, D_STATE, SSD_WIDTH), F32)],
        scratch_shapes=[
            pltpu.VMEM((seq + 2 * SUBLANES, XBC_WIDTH), F32),
            pltpu.VMEM((seq, XBC_WIDTH), F32),
            pltpu.VMEM((seq, LANES), F32),
            pltpu.VMEM((seq, LANES), F32),
            pltpu.VMEM((n_chunks, D_STATE, SSD_WIDTH), F32),
            pltpu.VMEM((D_STATE, SSD_WIDTH), F32),
            pltpu.VMEM((D_STATE, SSD_WIDTH), F32),
        ],
        compiler_params=_cparams(("arbitrary",)),
        name="ssd",
    )(xbc, z, dt, init, conv_w, conv_b, dt_bias, a_log, d_skip, ssd_norm_w, tri, expand)


def _first_argmax(rows):
    best_v = rows[0]
    best_i = jnp.zeros(rows[0].shape, jnp.int32)
    for i in range(1, len(rows)):
        better = rows[i] > best_v
        best_v = jnp.where(better, rows[i], best_v)
        best_i = jnp.where(better, i, best_i)
    return best_i, best_v


def _outproj_kernel(attn_ref, ssd_ref, x_ref, w_ref, g1_ref, nw_ref, sh_ref, sc_ref, wr_ref, rb_ref,
                    x1_ref, h_ref, gates_ref):
    half = ATTN_WIDTH
    mix = _dot(attn_ref[...], w_ref[0:half, :]) + _dot(ssd_ref[...], w_ref[half:, :])
    x1 = x_ref[...] + g1_ref[...] * mix
    x1_ref[...] = x1
    ms = jnp.mean(x1 * x1, axis=-1, keepdims=True)
    y = x1 * lax.rsqrt(ms + EPS) * nw_ref[...]
    h = y * (1.0 + sc_ref[...]) + sh_ref[...]
    h_hi, h_lo = _split2(h)
    h_ref[...] = h_hi
    wr = wr_ref[...]
    l1 = _dot(h_hi, wr).T
    l2 = _dot(h_lo, wr).T
    ne = N_EXPERTS
    logits = l1[0:ne, :] + l1[ne:2 * ne, :] + l2[0:ne, :]
    scores = _sigmoid(logits)
    sel = scores + rb_ref[...]
    srow = [sel[e:e + 1, :] for e in range(ne)]
    group_score = []
    for g in range(N_EXPERT_GROUPS):
        r = srow[g * EXPERTS_PER_GROUP:(g + 1) * EXPERTS_PER_GROUP]
        pairs = [r[i] + r[j] for i in range(len(r)) for j in range(i + 1, len(r))]
        top2 = pairs[0]
        for p in pairs[1:]:
            top2 = jnp.maximum(top2, p)
        group_score.append(top2)
    best_group, _ = _first_argmax(group_score)
    eid = lax.broadcasted_iota(jnp.int32, sel.shape, 0)
    masked = jnp.where(lax.shift_right_logical(eid, GROUP_SHIFT) == best_group, sel, NEG_INF)
    i1, _ = _first_argmax([masked[e:e + 1, :] for e in range(ne)])
    masked2 = jnp.where(eid == i1, -jnp.inf, masked)
    i2, _ = _first_argmax([masked2[e:e + 1, :] for e in range(ne)])
    pick1 = eid == i1
    pick2 = eid == i2
    w1 = jnp.sum(jnp.where(pick1, scores, 0.0), axis=0, keepdims=True)
    w2 = jnp.sum(jnp.where(pick2, scores, 0.0), axis=0, keepdims=True)
    wsum = w1 + w2
    gates = jnp.where(pick1, w1 / wsum, 0.0) + jnp.where(pick2, w2 / wsum, 0.0)
    gid = lax.broadcasted_iota(jnp.int32, (N_EXPERT_GROUPS, gates.shape[1]), 0)
    onehot = jnp.where(gid == best_group, 1.0, 0.0)
    pad = jnp.zeros((LANES - ne - N_EXPERT_GROUPS, gates.shape[1]), F32)
    gates_ref[...] = jnp.concatenate([gates, onehot, pad], axis=0).T


def _out_projection(attn, ssd, x, w_out_b, norm_w, mods, wr_p, rb_col, layer, row_of_step):
    t = x.shape[0]
    tm = ROW_TILE
    row = lambda i: (i, 0)
    return pl.pallas_call(
        _outproj_kernel,
        grid=(t // tm,),
        in_specs=[
            pl.BlockSpec((tm, ATTN_WIDTH), row),
            pl.BlockSpec((tm, SSD_WIDTH), row),
            pl.BlockSpec((tm, D_MODEL), row),
            pl.BlockSpec((None, D_MODEL, D_MODEL), lambda i: (layer, 0, 0)),
            _mod_spec(layer, 2, row_of_step),
            pl.BlockSpec((None, 1, D_MODEL), lambda i: (layer, 0, 0)),
            _mod_spec(layer, 3, row_of_step),
            _mod_spec(layer, 4, row_of_step),
            pl.BlockSpec((D_MODEL, LANES), lambda i: (0, 0)),
            pl.BlockSpec((N_EXPERTS, 1), lambda i: (0, 0)),
        ],
        out_specs=[pl.BlockSpec((tm, D_MODEL), row), pl.BlockSpec((tm, D_MODEL), row),
                   pl.BlockSpec((tm, LANES), row)],
        out_shape=[jax.ShapeDtypeStruct((t, D_MODEL), F32), jax.ShapeDtypeStruct((t, D_MODEL), BF16),
                   jax.ShapeDtypeStruct((t, LANES), F32)],
        compiler_params=_cparams(("arbitrary",)),
        name="out_projection",
    )(attn, ssd, x, w_out_b, mods, norm_w, mods, mods, wr_p, rb_col)


def _moe_kernel(h_ref, gates_ref, x1_ref, g2_ref, wg_ref, wu_ref, wd_ref, fw_ref, tril_ref, triu_ref,
                o_ref, p_scr, acc_scr, seg_ref, *, final):
    tm = h_ref.shape[0]
    rows = p_scr.shape[0]
    g = gates_ref[...]
    lane = lax.broadcasted_iota(jnp.int32, g.shape, 1)
    onehot = jnp.where((lane >= GROUP_LANE) & (lane < GROUP_LANE + N_EXPERT_GROUPS), g, 0.0)
    cnt = _dot(tril_ref[...], onehot.astype(BF16))
    tot_i = cnt[tm - 1:tm, :].astype(jnp.int32)
    chunk0 = jnp.int32(0)
    offs = []
    for k in range(N_EXPERT_GROUPS):
        offs.append((chunk0 * MOE_BLOCK).astype(F32))
        chunk0 = chunk0 + (tot_i[0, GROUP_LANE + k] + (MOE_BLOCK - 1)) // MOE_BLOCK
        seg_ref[k] = chunk0
    n_chunks = chunk0

    lane1 = lax.broadcasted_iota(jnp.int32, (1, LANES), 1)
    off_row = functools.reduce(jnp.add, [jnp.where(lane1 == GROUP_LANE + k, offs[k], 0.0)
                                         for k in range(N_EXPERT_GROUPS)])
    pos_col = jnp.sum(onehot * (cnt + off_row - 1.0), axis=-1, keepdims=True)
    r_lane = lax.broadcasted_iota(jnp.int32, (tm, rows), 1).astype(F32)
    unsort = jnp.where(r_lane == pos_col, 1.0, 0.0).astype(BF16)
    onehot_t = g.T[GROUP_LANE:GROUP_LANE + SUBLANES, :]
    sub = lax.broadcasted_iota(jnp.int32, onehot_t.shape, 0)
    onehot_t = jnp.where(sub < N_EXPERT_GROUPS, onehot_t, 0.0)
    cnt_t = _dot(onehot_t.astype(BF16), triu_ref[...])
    off_t = functools.reduce(jnp.add, [jnp.where(sub == k, offs[k], 0.0) for k in range(N_EXPERT_GROUPS)])
    pos_row = jnp.sum(onehot_t * (cnt_t + off_t - 1.0), axis=0, keepdims=True)
    r_sub = lax.broadcasted_iota(jnp.int32, (rows, tm), 0).astype(F32)
    p_scr[...] = jnp.where(r_sub == pos_row, 1.0, 0.0).astype(BF16)
    acc_scr[...] = jnp.zeros(acc_scr.shape, acc_scr.dtype)
    g_hi, g_lo = _split2(g)

    def chunk(c, carry):
        @pl.when(c < n_chunks)
        def _():
            k = functools.reduce(jnp.add, [jnp.where(c >= seg_ref[i], 1, 0)
                                           for i in range(N_EXPERT_GROUPS - 1)])
            r0 = pl.multiple_of(c * MOE_BLOCK, 16)
            pc = p_scr[pl.ds(r0, MOE_BLOCK), :]
            xb = _dot(pc, h_ref[...]).astype(BF16)
            gsb = _dot(pc, g_hi) + _dot(pc, g_lo)
            lane_b = lax.broadcasted_iota(jnp.int32, gsb.shape, 1)
            y = None
            for j in range(EXPERTS_PER_GROUP):
                e = k * EXPERTS_PER_GROUP + j
                a = _dot(xb, wg_ref[e])
                u = _dot(xb, wu_ref[e])
                ge = jnp.sum(jnp.where(lane_b == e, gsb, 0.0), axis=-1, keepdims=True)
                part = _dot((a * _sigmoid(a) * u * ge).astype(BF16), wd_ref[e])
                y = part if y is None else y + part
            acc_scr[pl.ds(r0, MOE_BLOCK), :] = y.astype(acc_scr.dtype)

        return carry

    lax.fori_loop(0, MOE_CHUNKS, chunk, 0)

    x2 = x1_ref[...] + g2_ref[...] * _dot(unsort, acc_scr[...])
    if final:
        ms = jnp.mean(x2 * x2, axis=-1, keepdims=True)
        x2 = x2 * lax.rsqrt(ms + EPS) * fw_ref[...]
    o_ref[...] = x2


def _moe_constants(tm):
    r = np.arange(tm)
    tril = (r[None, :] <= r[:, None]).astype(np.float32)
    return jnp.asarray(tril, BF16), jnp.asarray(tril.T, BF16)


def _moe(h, gates, x1, mods, w_gate_b, w_up_b, w_down_b, final_w, layer, row_of_step, final):
    t = h.shape[0]
    tm = MOE_TILE
    tril, triu = _moe_constants(tm)
    row = lambda i: (i, 0)
    const = lambda i: (0, 0)
    resident = dict(pipeline_mode=pl.Buffered(1))
    return pl.pallas_call(
        functools.partial(_moe_kernel, final=final),
        grid=(t // tm,),
        in_specs=[
            pl.BlockSpec((tm, D_MODEL), row),
            pl.BlockSpec((tm, LANES), row),
            pl.BlockSpec((tm, D_MODEL), row),
            _mod_spec(layer, 5, row_of_step),
            pl.BlockSpec((None, N_EXPERTS, D_MODEL, D_FF), lambda i: (layer, 0, 0, 0), **resident),
            pl.BlockSpec((None, N_EXPERTS, D_MODEL, D_FF), lambda i: (layer, 0, 0, 0), **resident),
            pl.BlockSpec((None, N_EXPERTS, D_FF, D_MODEL), lambda i: (layer, 0, 0, 0), **resident),
            pl.BlockSpec((1, D_MODEL), const),
            pl.BlockSpec((tm, tm), const, **resident),
            pl.BlockSpec((tm, tm), const, **resident),
        ],
        out_specs=pl.BlockSpec((tm, D_MODEL), row),
        out_shape=jax.ShapeDtypeStruct((t, D_MODEL), F32),
        scratch_shapes=[
            pltpu.VMEM((MOE_ROWS, tm), BF16),
            pltpu.VMEM((MOE_ROWS, D_MODEL), BF16),
            pltpu.SMEM((N_EXPERT_GROUPS,), jnp.int32),
        ],
        compiler_params=_cparams(("arbitrary",)),
        name="moe_final" if final else "moe",
    )(h, gates, x1, mods, w_gate_b, w_up_b, w_down_b, final_w, tril, triu)


def _rope_tables(n_tokens):
    rows = n_tokens // GRID_W
    row = np.repeat(np.arange(rows), GRID_W)
    col = np.tile(np.arange(GRID_W), rows)
    n_freq = HEAD_DIM // 4
    inv_freq = ROPE_THETA ** (-np.arange(n_freq) / n_freq)
    ang = np.concatenate([row[:, None] * inv_freq, col[:, None] * inv_freq], axis=-1)
    cos, sin = np.cos(ang), np.sin(ang)
    return (jnp.asarray(np.tile(cos, (1, 4)), F32),
            jnp.asarray(np.concatenate([-sin, sin, -sin, sin], axis=-1), F32))


def kernel(x_prompt, x_sample, cache_k, cache_v, state_ssm, c, c_ctx, norm1_w, norm2_w, final_norm_w,
           w_ada, b_ada, w_in, conv_w, conv_b, attn_sink, dt_bias, a_log, d_skip, ssd_norm_w, w_out,
           w_router, router_bias, w_gate, w_up, w_down):
    bc, sc, d = x_prompt.shape
    bl, sl, _ = x_sample.shape
    depth = w_in.shape[0]
    assert d == D_MODEL and depth == DEPTH and bl + 1 <= MOD_ROWS
    assert sc % ROW_TILE == 0 or ROW_TILE % sc == 0
    assert sl % MOE_TILE == 0 and (bc * sc) % MOE_TILE == 0 and MOE_TILE % sc == 0

    cvec = jnp.zeros((MOD_ROWS, d), F32).at[0].set(c_ctx).at[1:1 + bl].set(c)
    mods = _adaln_all(cvec, w_ada, b_ada).reshape(depth, MOD_ROWS, 6, 1, d)

    w_out_b = w_out.astype(BF16)
    w_gate_b, w_up_b, w_down_b = w_gate.astype(BF16), w_up.astype(BF16), w_down.astype(BF16)
    wr_hi = w_router.astype(BF16)
    wr_lo = (w_router - wr_hi.astype(F32)).astype(BF16)
    wr_p = jnp.concatenate([wr_hi, wr_lo, jnp.zeros((d, LANES - 2 * N_EXPERTS), BF16)], axis=-1)
    rb_col = router_bias.reshape(N_EXPERTS, 1)
    n1 = norm1_w.reshape(depth, 1, d)
    n2 = norm2_w.reshape(depth, 1, d)
    fw = final_norm_w.reshape(1, d)
    conv_w_p = jnp.concatenate([conv_w, jnp.zeros((depth, SUBLANES - D_CONV, XBC_WIDTH), F32)], axis=1)
    conv_b_p = conv_b.reshape(depth, 1, XBC_WIDTH)
    pad16 = lambda a: jnp.concatenate(
        [a.reshape(depth, 1, 2 * SSD_HEADS), jnp.zeros((depth, 1, LANES - 2 * SSD_HEADS), F32)], axis=-1)
    dtb_p = pad16(dt_bias)
    alog_p = pad16(a_log)
    dsk_p = jnp.repeat(d_skip, SSD_HEADDIM, axis=-1).reshape(depth, 1, SSD_WIDTH)
    snw_p = ssd_norm_w.reshape(depth, 1, SSD_WIDTH)
    rope_tabs = _rope_tables(sl)
    ck = cache_k.reshape(bl, depth, cache_k.shape[2], KV_WIDTH)
    cv = cache_v.reshape(bl, depth, cache_v.shape[2], KV_WIDTH)
    st_in = jnp.transpose(state_ssm, (0, 1, 2, 5, 3, 4)).reshape(bl, depth, 2, D_STATE, SSD_WIDTH)
    zero_state = jnp.zeros((bc, 2, D_STATE, SSD_WIDTH), F32)

    ctx_row = lambda i: 0
    lat_row_proj = lambda i: 1 + i // (sl // ROW_TILE)
    lat_row_moe = lambda i: 1 + i // (sl // MOE_TILE)

    xp = x_prompt.reshape(bc * sc, d)
    xs = x_sample.reshape(bl * sl, d)
    new_k, new_v, new_s = [], [], []
    for l in range(depth):
        last = l == depth - 1
        q, k, v, xbc, z, dt = _in_projection(xp, n1, mods, w_in, l, ctx_row, None, sc)
        attn = _attention_context(q, k, v, attn_sink, l, bc, sc)
        ssd, s_ctx = _ssd(xbc, z, dt, zero_state, conv_w_p, conv_b_p, dtb_p, alog_p, dsk_p, snw_p, l, bc, sc)
        x1, h2, gates = _out_projection(attn, ssd, xp, w_out_b, n2, mods, wr_p, rb_col, l, ctx_row)
        xp = _moe(h2, gates, x1, mods, w_gate_b, w_up_b, w_down_b, fw, l, ctx_row, last)
        new_k.append(k.reshape(bc, sc, N_KV_HEADS, HEAD_DIM))
        new_v.append(v.reshape(bc, sc, N_KV_HEADS, HEAD_DIM))
        new_s.append(jnp.transpose(s_ctx.reshape(bc, 2, D_STATE, SSD_HEADS, SSD_HEADDIM), (0, 1, 3, 4, 2)))
        q, k, v, xbc, z, dt = _in_projection(xs, n1, mods, w_in, l, lat_row_proj, rope_tabs, sl)
        attn = _attention_latent(q, k, v, ck, cv, attn_sink, l, bl, sl)
        ssd, _ = _ssd(xbc, z, dt, st_in[:, l], conv_w_p, conv_b_p, dtb_p, alog_p, dsk_p, snw_p, l, bl, sl)
        x1, h2, gates = _out_projection(attn, ssd, xs, w_out_b, n2, mods, wr_p, rb_col, l, lat_row_proj)
        xs = _moe(h2, gates, x1, mods, w_gate_b, w_up_b, w_down_b, fw, l, lat_row_moe, last)
    return (xp.reshape(bc, sc, d), xs.reshape(bl, sl, d),
            jnp.stack(new_k, axis=1), jnp.stack(new_v, axis=1), jnp.stack(new_s, axis=1))
```

```python
import functools

import numpy as np
import jax
import jax.numpy as jnp
from jax import lax
from jax.experimental import pallas as pl
from jax.experimental.pallas import tpu as pltpu

F32 = jnp.float32
BF16 = jnp.bfloat16

D_MODEL = 1024
DEPTH = 4
GRID_W = 64
EPS = 1e-6
NEG_INF = -1e30
N_HEADS = 8
N_KV_HEADS = 2
GQA_GROUP = N_HEADS // N_KV_HEADS
HEAD_DIM = 64
ATTN_WIDTH = N_HEADS * HEAD_DIM
KV_WIDTH = N_KV_HEADS * HEAD_DIM
BLOCK = 128
ROPE_THETA = 10000.0
SSD_HEADS = 8
SSD_HEADDIM = 64
SSD_WIDTH = SSD_HEADS * SSD_HEADDIM
SSD_GROUPS = 2
D_STATE = 64
BC_WIDTH = SSD_GROUPS * D_STATE
D_CONV = 5
CHUNK = 128
XBC_WIDTH = SSD_WIDTH + 2 * BC_WIDTH
N_EXPERTS = 16
N_EXPERT_GROUPS = 4
EXPERTS_PER_GROUP = N_EXPERTS // N_EXPERT_GROUPS
D_FF = 256
GROUP_SHIFT = EXPERTS_PER_GROUP.bit_length() - 1
assert 1 << GROUP_SHIFT == EXPERTS_PER_GROUP

LANES = 128
SUBLANES = 8
VMEM_LIMIT = 56 * 1024 * 1024

O_Q = 0
O_K = O_Q + ATTN_WIDTH
O_V = O_K + KV_WIDTH
O_XBC = O_V + KV_WIDTH
O_Z = O_XBC + XBC_WIDTH
O_DT = O_Z + SSD_WIDTH
IN_PAD = O_DT + LANES
MOD_ROWS = 16
ROW_TILE = 1024
MOE_TILE = 512
MOE_BLOCK = 144
MOE_CHUNKS = MOE_TILE // MOE_BLOCK + N_EXPERT_GROUPS
MOE_ROWS = -(-MOE_CHUNKS * MOE_BLOCK // LANES) * LANES
GROUP_LANE = N_EXPERTS
GROUP_COUNT_SHIFT = N_EXPERT_GROUPS.bit_length() - 1
assert 1 << GROUP_COUNT_SHIFT == N_EXPERT_GROUPS
SSD_UNROLL = 4
LAT_QBLOCKS = 8
LOG2E = 1.4426950408889634


def _cparams(sem):
    return pltpu.CompilerParams(dimension_semantics=sem, vmem_limit_bytes=VMEM_LIMIT)


def _sigmoid(x):
    return 1.0 / (1.0 + jnp.exp(-x))


def _split2(x):
    hi = x.astype(BF16)
    lo = (x - hi.astype(F32)).astype(BF16)
    return hi, lo


def _split3(x):
    hi = x.astype(BF16)
    r = x - hi.astype(F32)
    mid = r.astype(BF16)
    lo = (r - mid.astype(F32)).astype(BF16)
    return hi, mid, lo


def _dot(a, b):
    return jnp.dot(a, b, preferred_element_type=F32)


def _dot_nt(a, b):
    return lax.dot_general(a, b, (((1,), (1,)), ((), ())), preferred_element_type=F32)


def _adaln_kernel(c_ref, w_ref, b_ref, o_ref):
    c = c_ref[...]
    s = c * _sigmoid(c)
    s_hi, s_lo = _split2(s)
    w = w_ref[...]
    w_hi, w_lo = _split2(w)
    acc = _dot(s_hi, w_hi) + _dot(s_lo, w_hi) + _dot(s_hi, w_lo)
    o_ref[...] = acc + b_ref[...]


def _adaln_all(cvec, w_ada, b_ada):
    depth, d, n = w_ada.shape
    tn = 1024
    return pl.pallas_call(
        _adaln_kernel,
        grid=(depth, n // tn),
        in_specs=[
            pl.BlockSpec((MOD_ROWS, d), lambda l, j: (0, 0)),
            pl.BlockSpec((None, d, tn), lambda l, j: (l, 0, j)),
            pl.BlockSpec((None, 1, tn), lambda l, j: (l, 0, j)),
        ],
        out_specs=pl.BlockSpec((None, MOD_ROWS, tn), lambda l, j: (l, 0, j)),
        out_shape=jax.ShapeDtypeStruct((depth, MOD_ROWS, n), F32),
        compiler_params=_cparams(("arbitrary", "arbitrary")),
        name="adaln",
    )(cvec, w_ada, b_ada.reshape(depth, 1, n))


def _mod_spec(layer, which, row_of_step):
    return pl.BlockSpec((None, None, None, 1, D_MODEL),
                        lambda i, *_: (layer, row_of_step(i), which, 0, 0))


def _rope(x, cos4, sin4):
    lane = lax.broadcasted_iota(jnp.int32, x.shape, 1)
    first_half = (lane & (HEAD_DIM - 1)) < (HEAD_DIM // 2)
    partner = jnp.where(first_half,
                        pltpu.roll(x, LANES - HEAD_DIM // 2, 1),
                        pltpu.roll(x, HEAD_DIM // 2, 1))
    return x * cos4 + partner * sin4


def _inproj_kernel(*refs, rope):
    if rope:
        (x_ref, nw_ref, sh_ref, sc_ref, w_ref, cos_ref, sin_ref,
         q_ref, k_ref, v_ref, xbc_ref, z_ref, dt_ref, wb_scr) = refs
    else:
        (x_ref, nw_ref, sh_ref, sc_ref, w_ref,
         q_ref, k_ref, v_ref, xbc_ref, z_ref, dt_ref, wb_scr) = refs

    @pl.when(pl.program_id(0) == 0)
    def _():
        w = w_ref[...]
        wb_scr[:, O_Q:O_K] = (w[:, O_Q:O_K] * (HEAD_DIM ** -0.5 * LOG2E)).astype(BF16)
        wb_scr[:, O_K:O_DT] = w[:, O_K:O_DT].astype(BF16)
        pad = jnp.zeros((w.shape[0], IN_PAD - w.shape[1]), F32)
        wb_scr[:, O_DT:IN_PAD] = jnp.concatenate([w[:, O_DT:], pad], axis=-1).astype(BF16)

    x = x_ref[...]
    ms = jnp.mean(x * x, axis=-1, keepdims=True)
    y = x * lax.rsqrt(ms + EPS) * nw_ref[...]
    h = (y * (1.0 + sc_ref[...]) + sh_ref[...]).astype(BF16)

    def seg(a, b):
        return _dot(h, wb_scr[:, a:b])

    qkv = seg(O_Q, O_XBC)
    q, k, v = qkv[:, O_Q:O_K], qkv[:, O_K:O_V], qkv[:, O_V:O_XBC]
    if rope:
        cos4 = cos_ref[...]
        sin4 = sin_ref[...]
        for j in range(ATTN_WIDTH // LANES):
            q_ref[:, j * LANES:(j + 1) * LANES] = _rope(
                q[:, j * LANES:(j + 1) * LANES], cos4, sin4).astype(q_ref.dtype)
        k_ref[...] = _rope(k, cos4, sin4)
    else:
        q_ref[...] = q.astype(q_ref.dtype)
        k_ref[...] = k
    v_ref[...] = v
    xbc_ref[...] = seg(O_XBC, O_Z)
    zdt = seg(O_Z, IN_PAD)
    z_ref[...] = zdt[:, 0:SSD_WIDTH]
    dt_ref[...] = zdt[:, SSD_WIDTH:]


def _in_projection(x, norm_w, mods, w_in, layer, row_of_step, rope_tabs, seq_len):
    t = x.shape[0]
    tm = ROW_TILE
    rope = rope_tabs is not None
    row = lambda i: (i, 0)
    in_specs = [
        pl.BlockSpec((tm, D_MODEL), row),
        pl.BlockSpec((None, 1, D_MODEL), lambda i: (layer, 0, 0)),
        _mod_spec(layer, 0, row_of_step),
        _mod_spec(layer, 1, row_of_step),
        pl.BlockSpec((None, D_MODEL, w_in.shape[2]), lambda i: (layer, 0, 0), pipeline_mode=pl.Buffered(1)),
    ]
    args = [x, norm_w, mods, mods, w_in]
    if rope:
        steps_per_seq = seq_len // tm
        tab = pl.BlockSpec((tm, LANES), lambda i: (i % steps_per_seq, 0))
        in_specs += [tab, tab]
        args += list(rope_tabs)
    widths = (ATTN_WIDTH, KV_WIDTH, KV_WIDTH, XBC_WIDTH, SSD_WIDTH, LANES)
    dtypes = (BF16, F32, F32, F32, F32, F32)
    return pl.pallas_call(
        functools.partial(_inproj_kernel, rope=rope),
        grid=(t // tm,),
        in_specs=in_specs,
        out_specs=[pl.BlockSpec((tm, w), row) for w in widths],
        out_shape=[jax.ShapeDtypeStruct((t, w), dt) for w, dt in zip(widths, dtypes)],
        scratch_shapes=[pltpu.VMEM((D_MODEL, IN_PAD), BF16)],
        compiler_params=_cparams(("arbitrary",)),
        name="in_projection_rope" if rope else "in_projection",
    )(*args)


def _kv_operands(k, v):
    low = lax.broadcasted_iota(jnp.int32, k.shape, 1) < HEAD_DIM
    k_sw = pltpu.roll(k, HEAD_DIM, 1)
    v_sw = pltpu.roll(v, HEAD_DIM, 1)
    kdup = (jnp.where(low, k, k_sw).astype(BF16), jnp.where(low, k_sw, k).astype(BF16))
    vdup = (jnp.where(low, v, v_sw).astype(BF16), jnp.where(low, v_sw, v).astype(BF16))
    return kdup, vdup


def _attend_group(sink_ref, q_ref, o_ref, layer, g, r0, tq, segments):
    _attend_slabs(sink_ref, q_ref, o_ref, layer, g, range(GQA_GROUP // 2), r0, tq, segments)


def _attend_slabs(sink_ref, q_ref, o_ref, layer, g, slabs, r0, tq, segments):
    low = lax.broadcasted_iota(jnp.int32, (tq, LANES), 1) < HEAD_DIM
    base = g * GQA_GROUP * HEAD_DIM
    n_heads = 2 * len(slabs)
    rows = []
    for j in slabs:
        slab = q_ref[r0:r0 + tq, base + j * LANES:base + (j + 1) * LANES]
        zero = jnp.zeros_like(slab)
        rows += [jnp.where(low, slab, zero), jnp.where(low, zero, slab)]
    lhs = jnp.concatenate(rows, axis=0)
    sink = jnp.concatenate(
        [jnp.full((tq, LANES), sink_ref[layer, g * GQA_GROUP + 2 * j + i] * LOG2E, F32)
         for j in slabs for i in range(2)], axis=0)
    tiles = []
    for kdup, _, patches in segments:
        s = _dot_nt(lhs, kdup)
        ts = [s[:, c * LANES:(c + 1) * LANES] for c in range(s.shape[1] // LANES)]
        for c, mask in patches or ():
            ts[c] = (ts[c].reshape(n_heads, tq, LANES) + mask[None]).reshape(ts[c].shape)
        tiles.append(ts)
    flat = [t for ts in tiles for t in ts]
    m = jnp.max(functools.reduce(jnp.maximum, flat + [sink]), axis=-1, keepdims=True)
    probs = [[jnp.exp2(t - m) for t in ts] for ts in tiles]
    total = functools.reduce(jnp.add, [p for ps in probs for p in ps])
    lane0 = lax.broadcasted_iota(jnp.int32, sink.shape, 1) == 0
    total = total + jnp.where(lane0, jnp.exp2(sink - m), 0.0)
    denom = jnp.sum(total, axis=-1, keepdims=True)
    o = None
    for ps, (_, vdup, _) in zip(probs, segments):
        part = _dot(jnp.concatenate(ps, axis=-1).astype(BF16), vdup)
        o = part if o is None else o + part
    o = o * (1.0 / denom)
    for n, j in enumerate(slabs):
        pair = jnp.where(low, o[2 * n * tq:(2 * n + 1) * tq], o[(2 * n + 1) * tq:(2 * n + 2) * tq])
        o_ref[r0:r0 + tq, base + j * LANES:base + (j + 1) * LANES] = pair.astype(o_ref.dtype)


def _attn_ctx_kernel(sink_ref, q_ref, k_ref, v_ref, o_ref, *, layer):
    kdup, vdup = _kv_operands(k_ref[...], v_ref[...])
    for x in range(q_ref.shape[0] // BLOCK):
        for g in range(N_KV_HEADS):
            _attend_group(sink_ref, q_ref, o_ref, layer, g, x * BLOCK, BLOCK, [(kdup[g], vdup[g], None)])


def _attention_context(q, k, v, attn_sink, layer, batch, seq):
    row = lambda b: (b, 0)
    return pl.pallas_call(
        functools.partial(_attn_ctx_kernel, layer=layer),
        grid=(batch,),
        in_specs=[
            pl.BlockSpec(memory_space=pltpu.SMEM),
            pl.BlockSpec((seq, ATTN_WIDTH), row),
            pl.BlockSpec((seq, KV_WIDTH), row),
            pl.BlockSpec((seq, KV_WIDTH), row),
        ],
        out_specs=pl.BlockSpec((seq, ATTN_WIDTH), row),
        out_shape=jax.ShapeDtypeStruct((batch * seq, ATTN_WIDTH), BF16),
        compiler_params=_cparams(("arbitrary",)),
        name="attention_context",
    )(attn_sink, q, k, v)


def _attn_lat_kernel(sink_ref, q_ref, k_ref, v_ref, kc_ref, vc_ref, o_ref, *, layer, n_blocks):
    first_block = pl.program_id(1) * LAT_QBLOCKS
    band_blocks = LAT_QBLOCKS + 2

    def band(seq_ref):
        parts = []
        for i in range(band_blocks):
            blk = jnp.clip(first_block - 1 + i, 0, n_blocks - 1)
            parts.append(seq_ref[pl.ds(pl.multiple_of(blk * BLOCK, BLOCK), BLOCK), :])
        return jnp.concatenate(parts, axis=0)

    past_tiles = kc_ref.shape[0] // BLOCK
    kdup, vdup = _kv_operands(jnp.concatenate([kc_ref[...], band(k_ref)], axis=0),
                              jnp.concatenate([vc_ref[...], band(v_ref)], axis=0))
    qi = lax.broadcasted_iota(jnp.int32, (BLOCK, BLOCK), 0)
    kj = lax.broadcasted_iota(jnp.int32, (BLOCK, BLOCK), 1)
    for x in range(LAT_QBLOCKS):
        blk = first_block + x
        mask_prev = jnp.where(kj >= qi + jnp.where(blk > 0, 0, BLOCK), 0.0, NEG_INF)
        mask_next = jnp.where(kj <= qi - jnp.where(blk < n_blocks - 1, 0, BLOCK), 0.0, NEG_INF)
        patches = [(past_tiles, mask_prev), (past_tiles + 2, mask_next)]
        for g in range(N_KV_HEADS):
            if LAT_QBLOCKS == 1:
                keys, vals = kdup[g], vdup[g]
            else:
                pick = lambda a: jnp.concatenate(
                    [a[:past_tiles * BLOCK], a[(past_tiles + x) * BLOCK:(past_tiles + x + 3) * BLOCK]], axis=0)
                keys, vals = pick(kdup[g]), pick(vdup[g])
            _attend_group(sink_ref, q_ref, o_ref, layer, g, x * BLOCK, BLOCK, [(keys, vals, patches)])


def _attention_latent(q, k, v, cache_k, cache_v, attn_sink, layer, batch, seq):
    n_blocks = seq // BLOCK
    steps = n_blocks // LAT_QBLOCKS
    tq = LAT_QBLOCKS * BLOCK
    past = cache_k.shape[2]
    ctx = pl.BlockSpec((None, None, past, KV_WIDTH), lambda b, n: (b, layer, 0, 0))
    whole = pl.BlockSpec((seq, KV_WIDTH), lambda b, n: (b, 0))
    qo = pl.BlockSpec((tq, ATTN_WIDTH), lambda b, n: (b * steps + n, 0))
    return pl.pallas_call(
        functools.partial(_attn_lat_kernel, layer=layer, n_blocks=n_blocks),
        grid=(batch, steps),
        in_specs=[pl.BlockSpec(memory_space=pltpu.SMEM), qo, whole, whole, ctx, ctx],
        out_specs=qo,
        out_shape=jax.ShapeDtypeStruct((batch * seq, ATTN_WIDTH), BF16),
        compiler_params=_cparams(("arbitrary", "arbitrary")),
        name="attention_latent",
    )(attn_sink, q, k, v, cache_k, cache_v)


def _ssd_kernel(xbc_ref, z_ref, dt_ref, init_ref, cw_ref, cb_ref, dtb_ref, alog_ref, dsk_ref, nw_ref,
                tri_ref, exp_ref, *rest, seq, emit_states):
    if emit_states:
        _, y_ref, st_ref, pad_scr, act_scr, dtv_scr, cs_scr, sb_scr, sf_scr, sr_scr = rest
    else:
        y_ref, pad_scr, act_scr, dtv_scr, cs_scr, sb_scr, sf_scr, sr_scr = rest
    n_chunks = seq // CHUNK
    half = SSD_WIDTH // SSD_GROUPS
    pad_lo = SUBLANES
    pad_scr[0:pad_lo, :] = jnp.zeros((pad_lo, XBC_WIDTH), F32)
    pad_scr[pad_lo:pad_lo + seq, :] = xbc_ref[...]
    pad_scr[pad_lo + seq:2 * pad_lo + seq, :] = jnp.zeros((pad_lo, XBC_WIDTH), F32)

    lane = lax.broadcasted_iota(jnp.int32, (CHUNK, LANES), 1)
    is_fwd = lane < SSD_HEADS
    is_head = lane < 2 * SSD_HEADS
    ti = lax.broadcasted_iota(jnp.int32, (CHUNK, CHUNK), 0)
    tj = lax.broadcasted_iota(jnp.int32, (CHUNK, CHUNK), 1)
    lower = tj <= ti
    upper = tj >= ti
    a_row = jnp.where(is_head[0:1, :], -jnp.exp(alog_ref[...]), 0.0)

    def conv_silu(r0):
        win = pad_scr[pl.ds(r0, CHUNK + 2 * pad_lo), :]
        first = pad_lo - (D_CONV - 1) // 2
        n_win = CHUNK + 2 * pad_lo
        tap = lambda kk: pltpu.roll(win, (n_win - first - kk) % n_win, 0)[0:CHUNK, :]
        acc = cb_ref[...] + cw_ref[0:1, :] * tap(0)
        for kk in range(1, D_CONV):
            acc = acc + cw_ref[kk:kk + 1, :] * tap(kk)
        return acc * _sigmoid(acc)

    def step_sizes(r0):
        x = dt_ref[pl.ds(r0, CHUNK), :] + dtb_ref[...]
        dtv = jnp.maximum(x, 0.0) + jnp.log(1.0 + jnp.exp(-jnp.abs(x)))
        hi, mid, lo = _split3(dtv * a_row)
        tri = tri_ref[...]
        cs = _dot(tri, hi) + _dot(tri, mid) + _dot(tri, lo)
        dtv_scr[pl.ds(r0, CHUNK), :] = dtv
        cs_scr[pl.ds(r0, CHUNK), :] = cs
        return dtv, cs

    def prep(dtv, cs):
        ecs = cs - dtv * a_row
        tot = cs[CHUNK - 1:CHUNK, :]
        ldt = jnp.log(dtv)
        col_q = jnp.where(is_fwd, cs, ecs)
        row_q = jnp.where(is_fwd, cs - ldt, ecs + ldt).T
        scale_y = jnp.where(is_fwd, jnp.exp(cs), jnp.exp(tot - ecs))
        scale_s = dtv * jnp.where(is_fwd, jnp.exp(tot - cs), jnp.exp(ecs))
        ex = exp_ref[...]
        scale_y = _dot(scale_y.astype(BF16), ex)
        scale_s = _dot(scale_s.astype(BF16), ex)
        d_hi, d_mid, d_lo = _split3(jnp.broadcast_to(jnp.exp(tot), (SUBLANES, LANES)))
        decay = (_dot(d_hi, ex) + _dot(d_mid, ex) + _dot(d_lo, ex))[0:1, :]
        return col_q, row_q, scale_y, scale_s, decay

    def chunk_state(act, weights, decay, state):
        xw = (act[:, 0:SSD_WIDTH] * weights).astype(BF16)
        bt = act[:, SSD_WIDTH:SSD_WIDTH + BC_WIDTH].T.astype(BF16)
        upd = jnp.concatenate(
            [_dot(bt[g * D_STATE:(g + 1) * D_STATE, :], xw[:, g * half:(g + 1) * half])
             for g in range(SSD_GROUPS)], axis=-1)
        return decay * state + upd

    sr_scr[...] = init_ref[1]

    def bwd_body(i, carry):
        c = n_chunks - 1 - i
        r0 = pl.multiple_of(c * CHUNK, CHUNK)
        act = conv_silu(r0)
        act_scr[pl.ds(r0, CHUNK), :] = act
        _, _, _, scale_s, decay = prep(*step_sizes(r0))
        state = sr_scr[...]
        sb_scr[c] = state
        sr_scr[...] = chunk_state(act, scale_s[:, SSD_WIDTH:], decay[:, SSD_WIDTH:], state)
        return carry

    lax.fori_loop(0, n_chunks, bwd_body, 0, unroll=min(SSD_UNROLL, n_chunks))
    if emit_states:
        st_ref[1] = sr_scr[...].T

    sf_scr[...] = init_ref[0]

    def fwd_body(c, carry):
        r0 = pl.multiple_of(c * CHUNK, CHUNK)
        act = act_scr[pl.ds(r0, CHUNK), :]
        col_q, row_q, scale_y, scale_s, decay = prep(dtv_scr[pl.ds(r0, CHUNK), :], cs_scr[pl.ds(r0, CHUNK), :])
        xs = act[:, 0:SSD_WIDTH]
        xs_b = xs.astype(BF16)
        bmat = act[:, SSD_WIDTH:SSD_WIDTH + BC_WIDTH].astype(BF16)
        cmat = act[:, SSD_WIDTH + BC_WIDTH:XBC_WIDTH].astype(BF16)
        s_f = sf_scr[...]
        s_b = sb_scr[c]
        y_parts = []
        for g in range(SSD_GROUPS):
            gs = slice(g * D_STATE, (g + 1) * D_STATE)
            cb = _dot_nt(cmat[:, gs], bmat[:, gs])
            for pair in range(SSD_HEADS // SSD_GROUPS // 2):
                ws = []
                for h in (g * 4 + 2 * pair, g * 4 + 2 * pair + 1):
                    hb = SSD_HEADS + h
                    e_f = jnp.exp(jnp.where(lower, col_q[:, h:h + 1] - row_q[h:h + 1, :], NEG_INF))
                    e_b = jnp.exp(jnp.where(upper, row_q[hb:hb + 1, :] - col_q[:, hb:hb + 1], NEG_INF))
                    ws.append((cb * (e_f + e_b)).astype(BF16))
                h0 = g * 4 + 2 * pair
                slab = xs_b[:, h0 * SSD_HEADDIM:(h0 + 2) * SSD_HEADDIM]
                first = lane < SSD_HEADDIM
                rhs = jnp.concatenate([jnp.where(first, slab, jnp.zeros_like(slab)),
                                       jnp.where(first, jnp.zeros_like(slab), slab)], axis=0)
                y_parts.append(_dot(jnp.concatenate(ws, axis=-1), rhs))
        y = jnp.concatenate(y_parts, axis=-1)
        off_f = jnp.concatenate(
            [_dot(cmat[:, g * D_STATE:(g + 1) * D_STATE], s_f[:, g * half:(g + 1) * half].astype(BF16))
             for g in range(SSD_GROUPS)], axis=-1)
        off_b = jnp.concatenate(
            [_dot(cmat[:, g * D_STATE:(g + 1) * D_STATE], s_b[:, g * half:(g + 1) * half].astype(BF16))
             for g in range(SSD_GROUPS)], axis=-1)
        y = y + off_f * scale_y[:, 0:SSD_WIDTH] + off_b * scale_y[:, SSD_WIDTH:] + dsk_ref[...] * xs
        zc = z_ref[pl.ds(r0, CHUNK), :]
        gated = y * (zc * _sigmoid(zc))
        outs = []
        for g in range(SSD_GROUPS):
            gg = gated[:, g * half:(g + 1) * half]
            outs.append(gg * lax.rsqrt(jnp.mean(gg * gg, axis=-1, keepdims=True) + EPS))
        y_ref[pl.ds(r0, CHUNK), :] = (jnp.concatenate(outs, axis=-1) * nw_ref[...]).astype(y_ref.dtype)
        sf_scr[...] = chunk_state(act, scale_s[:, 0:SSD_WIDTH], decay[:, 0:SSD_WIDTH], s_f)
        return carry

    lax.fori_loop(0, n_chunks, fwd_body, 0, unroll=min(SSD_UNROLL, n_chunks))
    if emit_states:
        st_ref[0] = sf_scr[...].T


def _ssd_constants():
    r = np.arange(CHUNK)
    tri = (r[None, :] <= r[:, None]).astype(np.float32)
    cols = np.arange(2 * SSD_WIDTH)
    expand = (cols[None, :] // SSD_HEADDIM == np.arange(LANES)[:, None]).astype(np.float32)
    return jnp.asarray(tri, BF16), jnp.asarray(expand, BF16)


def _ssd(xbc, z, dt, init, conv_w, conv_b, dt_bias, a_log, d_skip, ssd_norm_w, layer, batch, seq, states=None):
    tri, expand = _ssd_constants()
    n_chunks = seq // CHUNK
    emit_states = states is not None
    row = lambda b: (b, 0)
    lay = lambda b: (layer, 0, 0)
    const = lambda b: (0, 0)
    in_specs = [
        pl.BlockSpec((seq, XBC_WIDTH), row),
        pl.BlockSpec((seq, SSD_WIDTH), row),
        pl.BlockSpec((seq, LANES), row),
        pl.BlockSpec((None, 2, D_STATE, SSD_WIDTH), lambda b: (b, 0, 0, 0)),
        pl.BlockSpec((None, SUBLANES, XBC_WIDTH), lay),
        pl.BlockSpec((None, 1, XBC_WIDTH), lay),
        pl.BlockSpec((None, 1, LANES), lay),
        pl.BlockSpec((None, 1, LANES), lay),
        pl.BlockSpec((None, 1, SSD_WIDTH), lay),
        pl.BlockSpec((None, 1, SSD_WIDTH), lay),
        pl.BlockSpec((CHUNK, CHUNK), const),
        pl.BlockSpec((LANES, 2 * SSD_WIDTH), const),
    ]
    args = [xbc, z, dt, init, conv_w, conv_b, dt_bias, a_log, d_skip, ssd_norm_w, tri, expand]
    out_specs = [pl.BlockSpec((seq, SSD_WIDTH), row)]
    out_shape = [jax.ShapeDtypeStruct((batch * seq, SSD_WIDTH), BF16)]
    aliases = {}
    if emit_states:
        in_specs.append(pl.BlockSpec(memory_space=pl.ANY))
        args.append(states)
        out_specs.append(pl.BlockSpec((None, None, 2, SSD_WIDTH, D_STATE), lambda b: (b, layer, 0, 0, 0)))
        out_shape.append(jax.ShapeDtypeStruct(states.shape, states.dtype))
        aliases = {len(args) - 1: 1}
    return pl.pallas_call(
        functools.partial(_ssd_kernel, seq=seq, emit_states=emit_states),
        grid=(batch,),
        in_specs=in_specs,
        out_specs=out_specs,
        out_shape=out_shape,
        input_output_aliases=aliases,
        scratch_shapes=[
            pltpu.VMEM((seq + 2 * SUBLANES, XBC_WIDTH), F32),
            pltpu.VMEM((seq, XBC_WIDTH), F32),
            pltpu.VMEM((seq, LANES), F32),
            pltpu.VMEM((seq, LANES), F32),
            pltpu.VMEM((n_chunks, D_STATE, SSD_WIDTH), F32),
            pltpu.VMEM((D_STATE, SSD_WIDTH), F32),
            pltpu.VMEM((D_STATE, SSD_WIDTH), F32),
        ],
        compiler_params=_cparams(("arbitrary",)),
        name="ssd_states" if emit_states else "ssd",
    )(*args)


def _first_argmax(rows):
    best_v = rows[0]
    best_i = jnp.zeros(rows[0].shape, jnp.int32)
    for i in range(1, len(rows)):
        better = rows[i] > best_v
        best_v = jnp.where(better, rows[i], best_v)
        best_i = jnp.where(better, i, best_i)
    return best_i, best_v


def _outproj_kernel(attn_ref, ssd_ref, x_ref, w_ref, g1_ref, nw_ref, sh_ref, sc_ref, wr_ref, rb_ref,
                    x1_ref, h_ref, gates_ref):
    half = ATTN_WIDTH
    mix = _dot(attn_ref[...], w_ref[0:half, :]) + _dot(ssd_ref[...], w_ref[half:, :])
    x1 = x_ref[...] + g1_ref[...] * mix
    x1_ref[...] = x1
    ms = jnp.mean(x1 * x1, axis=-1, keepdims=True)
    y = x1 * lax.rsqrt(ms + EPS) * nw_ref[...]
    h = y * (1.0 + sc_ref[...]) + sh_ref[...]
    h_hi, h_lo = _split2(h)
    h_ref[...] = h_hi
    wr = wr_ref[...]
    l1 = _dot(h_hi, wr).T
    l2 = _dot(h_lo, wr).T
    ne = N_EXPERTS
    logits = l1[0:ne, :] + l1[ne:2 * ne, :] + l2[0:ne, :]
    scores = _sigmoid(logits)
    sel = scores + rb_ref[...]
    srow = [sel[e:e + 1, :] for e in range(ne)]
    group_score = []
    for g in range(N_EXPERT_GROUPS):
        r = srow[g * EXPERTS_PER_GROUP:(g + 1) * EXPERTS_PER_GROUP]
        pairs = [r[i] + r[j] for i in range(len(r)) for j in range(i + 1, len(r))]
        top2 = pairs[0]
        for p in pairs[1:]:
            top2 = jnp.maximum(top2, p)
        group_score.append(top2)
    best_group, _ = _first_argmax(group_score)
    eid = lax.broadcasted_iota(jnp.int32, sel.shape, 0)
    masked = jnp.where(lax.shift_right_logical(eid, GROUP_SHIFT) == best_group, sel, NEG_INF)
    i1, _ = _first_argmax([masked[e:e + 1, :] for e in range(ne)])
    masked2 = jnp.where(eid == i1, -jnp.inf, masked)
    i2, _ = _first_argmax([masked2[e:e + 1, :] for e in range(ne)])
    pick1 = eid == i1
    pick2 = eid == i2
    w1 = jnp.sum(jnp.where(pick1, scores, 0.0), axis=0, keepdims=True)
    w2 = jnp.sum(jnp.where(pick2, scores, 0.0), axis=0, keepdims=True)
    wsum = w1 + w2
    gates = jnp.where(pick1, w1 / wsum, 0.0) + jnp.where(pick2, w2 / wsum, 0.0)
    gid = lax.broadcasted_iota(jnp.int32, (N_EXPERT_GROUPS, gates.shape[1]), 0)
    onehot = jnp.where(gid == best_group, 1.0, 0.0)
    pad = jnp.zeros((LANES - ne - N_EXPERT_GROUPS, gates.shape[1]), F32)
    gates_ref[...] = jnp.concatenate([gates, onehot, pad], axis=0).T


def _out_projection(attn, ssd, x, w_out_b, norm_w, mods, wr_p, rb_col, layer, row_of_step):
    t = x.shape[0]
    tm = ROW_TILE
    row = lambda i: (i, 0)
    return pl.pallas_call(
        _outproj_kernel,
        grid=(t // tm,),
        in_specs=[
            pl.BlockSpec((tm, ATTN_WIDTH), row),
            pl.BlockSpec((tm, SSD_WIDTH), row),
            pl.BlockSpec((tm, D_MODEL), row),
            pl.BlockSpec((None, D_MODEL, D_MODEL), lambda i: (layer, 0, 0)),
            _mod_spec(layer, 2, row_of_step),
            pl.BlockSpec((None, 1, D_MODEL), lambda i: (layer, 0, 0)),
            _mod_spec(layer, 3, row_of_step),
            _mod_spec(layer, 4, row_of_step),
            pl.BlockSpec((D_MODEL, LANES), lambda i: (0, 0)),
            pl.BlockSpec((N_EXPERTS, 1), lambda i: (0, 0)),
        ],
        out_specs=[pl.BlockSpec((tm, D_MODEL), row), pl.BlockSpec((tm, D_MODEL), row),
                   pl.BlockSpec((tm, LANES), row)],
        out_shape=[jax.ShapeDtypeStruct((t, D_MODEL), F32), jax.ShapeDtypeStruct((t, D_MODEL), BF16),
                   jax.ShapeDtypeStruct((t, LANES), F32)],
        compiler_params=_cparams(("arbitrary",)),
        name="out_projection",
    )(attn, ssd, x, w_out_b, mods, norm_w, mods, mods, wr_p, rb_col)


def _moe_kernel(h_ref, gates_ref, x1_ref, g2_ref, wg_ref, wu_ref, wd_ref, fw_ref, tril_ref, triu_ref,
                o_ref, p_scr, acc_scr, seg_ref, *, final):
    tm = h_ref.shape[0]
    rows = p_scr.shape[0]
    g = gates_ref[...]
    lane = lax.broadcasted_iota(jnp.int32, g.shape, 1)
    onehot = jnp.where((lane >= GROUP_LANE) & (lane < GROUP_LANE + N_EXPERT_GROUPS), g, 0.0)
    cnt = _dot(tril_ref[...], onehot.astype(BF16))
    tot_i = cnt[tm - 1:tm, :].astype(jnp.int32)
    chunk0 = jnp.int32(0)
    offs = []
    for k in range(N_EXPERT_GROUPS):
        offs.append((chunk0 * MOE_BLOCK).astype(F32))
        chunk0 = chunk0 + (tot_i[0, GROUP_LANE + k] + (MOE_BLOCK - 1)) // MOE_BLOCK
        seg_ref[k] = chunk0
    n_chunks = chunk0

    lane1 = lax.broadcasted_iota(jnp.int32, (1, LANES), 1)
    off_row = functools.reduce(jnp.add, [jnp.where(lane1 == GROUP_LANE + k, offs[k], 0.0)
                                         for k in range(N_EXPERT_GROUPS)])
    pos_col = jnp.sum(onehot * (cnt + off_row - 1.0), axis=-1, keepdims=True)
    r_lane = lax.broadcasted_iota(jnp.int32, (tm, rows), 1).astype(F32)
    unsort = jnp.where(r_lane == pos_col, 1.0, 0.0).astype(BF16)
    onehot_t = g.T[GROUP_LANE:GROUP_LANE + SUBLANES, :]
    sub = lax.broadcasted_iota(jnp.int32, onehot_t.shape, 0)
    onehot_t = jnp.where(sub < N_EXPERT_GROUPS, onehot_t, 0.0)
    cnt_t = _dot(onehot_t.astype(BF16), triu_ref[...])
    off_t = functools.reduce(jnp.add, [jnp.where(sub == k, offs[k], 0.0) for k in range(N_EXPERT_GROUPS)])
    pos_row = jnp.sum(onehot_t * (cnt_t + off_t - 1.0), axis=0, keepdims=True)
    r_sub = lax.broadcasted_iota(jnp.int32, (rows, tm), 0).astype(F32)
    p_scr[...] = jnp.where(r_sub == pos_row, 1.0, 0.0).astype(BF16)
    acc_scr[...] = jnp.zeros(acc_scr.shape, acc_scr.dtype)
    g_hi, g_lo = _split2(g)

    def chunk(c, carry):
        @pl.when(c < n_chunks)
        def _():
            k = functools.reduce(jnp.add, [jnp.where(c >= seg_ref[i], 1, 0)
                                           for i in range(N_EXPERT_GROUPS - 1)])
            r0 = pl.multiple_of(c * MOE_BLOCK, 16)
            pc = p_scr[pl.ds(r0, MOE_BLOCK), :]
            xb = _dot(pc, h_ref[...]).astype(BF16)
            gsb = _dot(pc, g_hi) + _dot(pc, g_lo)
            lane_b = lax.broadcasted_iota(jnp.int32, gsb.shape, 1)
            y = None
            for j in range(EXPERTS_PER_GROUP):
                e = k * EXPERTS_PER_GROUP + j
                a = _dot(xb, wg_ref[e])
                u = _dot(xb, wu_ref[e])
                ge = jnp.sum(jnp.where(lane_b == e, gsb, 0.0), axis=-1, keepdims=True)
                part = _dot((a * _sigmoid(a) * u * ge).astype(BF16), wd_ref[e])
                y = part if y is None else y + part
            acc_scr[pl.ds(r0, MOE_BLOCK), :] = y.astype(acc_scr.dtype)

        return carry

    lax.fori_loop(0, MOE_CHUNKS, chunk, 0)

    x2 = x1_ref[...] + g2_ref[...] * _dot(unsort, acc_scr[...])
    if final:
        ms = jnp.mean(x2 * x2, axis=-1, keepdims=True)
        x2 = x2 * lax.rsqrt(ms + EPS) * fw_ref[...]
    o_ref[...] = x2


def _moe_constants(tm):
    r = np.arange(tm)
    tril = (r[None, :] <= r[:, None]).astype(np.float32)
    return jnp.asarray(tril, BF16), jnp.asarray(tril.T, BF16)


def _moe(h, gates, x1, mods, w_gate_b, w_up_b, w_down_b, final_w, layer, row_of_step, final):
    t = h.shape[0]
    tm = MOE_TILE
    tril, triu = _moe_constants(tm)
    row = lambda i: (i, 0)
    const = lambda i: (0, 0)
    resident = dict(pipeline_mode=pl.Buffered(1))
    return pl.pallas_call(
        functools.partial(_moe_kernel, final=final),
        grid=(t // tm,),
        in_specs=[
            pl.BlockSpec((tm, D_MODEL), row),
            pl.BlockSpec((tm, LANES), row),
            pl.BlockSpec((tm, D_MODEL), row),
            _mod_spec(layer, 5, row_of_step),
            pl.BlockSpec((None, N_EXPERTS, D_MODEL, D_FF), lambda i: (layer, 0, 0, 0), **resident),
            pl.BlockSpec((None, N_EXPERTS, D_MODEL, D_FF), lambda i: (layer, 0, 0, 0), **resident),
            pl.BlockSpec((None, N_EXPERTS, D_FF, D_MODEL), lambda i: (layer, 0, 0, 0), **resident),
            pl.BlockSpec((1, D_MODEL), const),
            pl.BlockSpec((tm, tm), const, **resident),
            pl.BlockSpec((tm, tm), const, **resident),
        ],
        out_specs=pl.BlockSpec((tm, D_MODEL), row),
        out_shape=jax.ShapeDtypeStruct((t, D_MODEL), F32),
        scratch_shapes=[
            pltpu.VMEM((MOE_ROWS, tm), BF16),
            pltpu.VMEM((MOE_ROWS, D_MODEL), BF16),
            pltpu.SMEM((N_EXPERT_GROUPS,), jnp.int32),
        ],
        compiler_params=_cparams(("arbitrary",)),
        name="moe_final" if final else "moe",
    )(h, gates, x1, mods, w_gate_b, w_up_b, w_down_b, final_w, tril, triu)


def _rope_tables(n_tokens):
    rows = n_tokens // GRID_W
    row = np.repeat(np.arange(rows), GRID_W)
    col = np.tile(np.arange(GRID_W), rows)
    n_freq = HEAD_DIM // 4
    inv_freq = ROPE_THETA ** (-np.arange(n_freq) / n_freq)
    ang = np.concatenate([row[:, None] * inv_freq, col[:, None] * inv_freq], axis=-1)
    cos, sin = np.cos(ang), np.sin(ang)
    return (jnp.asarray(np.tile(cos, (1, 4)), F32),
            jnp.asarray(np.concatenate([-sin, sin, -sin, sin], axis=-1), F32))


def kernel(x_prompt, x_sample, cache_k, cache_v, state_ssm, c, c_ctx, norm1_w, norm2_w, final_norm_w,
           w_ada, b_ada, w_in, conv_w, conv_b, attn_sink, dt_bias, a_log, d_skip, ssd_norm_w, w_out,
           w_router, router_bias, w_gate, w_up, w_down):
    bc, sc, d = x_prompt.shape
    bl, sl, _ = x_sample.shape
    depth = w_in.shape[0]
    assert d == D_MODEL and depth == DEPTH and bl + 1 <= MOD_ROWS
    assert sc % ROW_TILE == 0 or ROW_TILE % sc == 0
    assert sl % MOE_TILE == 0 and (bc * sc) % MOE_TILE == 0 and MOE_TILE % sc == 0

    cvec = jnp.zeros((MOD_ROWS, d), F32).at[0].set(c_ctx).at[1:1 + bl].set(c)
    mods = _adaln_all(cvec, w_ada, b_ada).reshape(depth, MOD_ROWS, 6, 1, d)

    w_out_b = w_out.astype(BF16)
    w_gate_b, w_up_b, w_down_b = w_gate.astype(BF16), w_up.astype(BF16), w_down.astype(BF16)
    wr_hi = w_router.astype(BF16)
    wr_lo = (w_router - wr_hi.astype(F32)).astype(BF16)
    wr_p = jnp.concatenate([wr_hi, wr_lo, jnp.zeros((d, LANES - 2 * N_EXPERTS), BF16)], axis=-1)
    rb_col = router_bias.reshape(N_EXPERTS, 1)
    n1 = norm1_w.reshape(depth, 1, d)
    n2 = norm2_w.reshape(depth, 1, d)
    fw = final_norm_w.reshape(1, d)
    conv_w_p = jnp.concatenate([conv_w, jnp.zeros((depth, SUBLANES - D_CONV, XBC_WIDTH), F32)], axis=1)
    conv_b_p = conv_b.reshape(depth, 1, XBC_WIDTH)
    pad16 = lambda a: jnp.concatenate(
        [a.reshape(depth, 1, 2 * SSD_HEADS), jnp.zeros((depth, 1, LANES - 2 * SSD_HEADS), F32)], axis=-1)
    dtb_p = pad16(dt_bias)
    alog_p = pad16(a_log)
    dsk_p = jnp.repeat(d_skip, SSD_HEADDIM, axis=-1).reshape(depth, 1, SSD_WIDTH)
    snw_p = ssd_norm_w.reshape(depth, 1, SSD_WIDTH)
    rope_tabs = _rope_tables(sl)
    ck = cache_k.reshape(bl, depth, cache_k.shape[2], KV_WIDTH)
    cv = cache_v.reshape(bl, depth, cache_v.shape[2], KV_WIDTH)
    st_in = jnp.transpose(state_ssm, (0, 1, 2, 5, 3, 4)).reshape(bl, depth, 2, D_STATE, SSD_WIDTH)
    zero_state = jnp.zeros((bc, 2, D_STATE, SSD_WIDTH), F32)

    ctx_row = lambda i: 0
    lat_row_proj = lambda i: 1 + i // (sl // ROW_TILE)
    lat_row_moe = lambda i: 1 + i // (sl // MOE_TILE)

    xp = x_prompt.reshape(bc * sc, d)
    xs = x_sample.reshape(bl * sl, d)
    new_k, new_v = [], []
    new_s = jnp.zeros((bc, depth, 2, SSD_WIDTH, D_STATE), F32)
    for l in range(depth):
        last = l == depth - 1
        q, k, v, xbc, z, dt = _in_projection(xp, n1, mods, w_in, l, ctx_row, None, sc)
        attn = _attention_context(q, k, v, attn_sink, l, bc, sc)
        ssd, new_s = _ssd(xbc, z, dt, zero_state, conv_w_p, conv_b_p, dtb_p, alog_p, dsk_p, snw_p, l, bc, sc,
                          states=new_s)
        x1, h2, gates = _out_projection(attn, ssd, xp, w_out_b, n2, mods, wr_p, rb_col, l, ctx_row)
        xp = _moe(h2, gates, x1, mods, w_gate_b, w_up_b, w_down_b, fw, l, ctx_row, last)
        new_k.append(k.reshape(bc, sc, N_KV_HEADS, HEAD_DIM))
        new_v.append(v.reshape(bc, sc, N_KV_HEADS, HEAD_DIM))
        q, k, v, xbc, z, dt = _in_projection(xs, n1, mods, w_in, l, lat_row_proj, rope_tabs, sl)
        attn = _attention_latent(q, k, v, ck, cv, attn_sink, l, bl, sl)
        (ssd,) = _ssd(xbc, z, dt, st_in[:, l], conv_w_p, conv_b_p, dtb_p, alog_p, dsk_p, snw_p, l, bl, sl)
        x1, h2, gates = _out_projection(attn, ssd, xs, w_out_b, n2, mods, wr_p, rb_col, l, lat_row_proj)
        xs = _moe(h2, gates, x1, mods, w_gate_b, w_up_b, w_down_b, fw, l, lat_row_moe, last)
    return (xp.reshape(bc, sc, d), xs.reshape(bl, sl, d),
            jnp.stack(new_k, axis=1), jnp.stack(new_v, axis=1),
            new_s.reshape(bc, depth, 2, SSD_HEADS, SSD_HEADDIM, D_STATE))
```
